```python
import math
import jax, jax.numpy as jnp
from jax import lax
import numpy as np


D_MODEL = 1024
BATCH = 16
SEQ = 2048
DEPTH = 1

HEAD_DIM = 64
N_HEADS_MOBA = 8
N_HEADS_SB = 8
MOBA_WIDTH = N_HEADS_MOBA * HEAD_DIM
SB_WIDTH = N_HEADS_SB * HEAD_DIM
N_BRANCHES = 2
IN_WIDTH = 3 * MOBA_WIDTH + 3 * SB_WIDTH + N_BRANCHES * D_MODEL
MOBA_BLOCK = 256
MOBA_TOPK = 3
MOBA_Q_CHUNK = 16
SB_Q_BLOCK = 128
D_FF = 2816
CONV_WIDTH = 3
REL_BUCKETS = 32
REL_MAX_DIST = 128
NORM_EPS = 1e-6
NEG_INF = -1e30

kernel_name = 'hybrid_moba_stickbreaking_convffn_block'


def rms_norm(x, g):
    xf = x.astype(jnp.float32)
    y = xf * lax.rsqrt(jnp.mean(xf * xf, axis=-1, keepdims=True) + NORM_EPS)
    return (y * g.astype(jnp.float32)).astype(x.dtype)


def modulate(h, shift, scale):
    return h * (1.0 + scale[:, None, :]) + shift[:, None, :]


def to_heads(t, n_heads):
    b, s, _ = t.shape
    return t.reshape(b, s, n_heads, HEAD_DIM).transpose(0, 2, 1, 3)


def merge_heads(t):
    b, h, s, hd = t.shape
    return t.transpose(0, 2, 1, 3).reshape(b, s, h * hd)


def t5_bucket(rel):
    n = jnp.maximum(rel, 0)
    max_exact = REL_BUCKETS // 2
    n_f = jnp.maximum(n, max_exact).astype(jnp.float32)
    large = max_exact + (jnp.log(n_f / max_exact) / math.log(REL_MAX_DIST / max_exact)
                         * (REL_BUCKETS - max_exact)).astype(jnp.int32)
    large = jnp.minimum(large, REL_BUCKETS - 1)
    return jnp.where(n < max_exact, n, large)


def moba_attention(q, k, v, rel_table):
    b, h, s, hd = q.shape
    n_blk = -(-s // MOBA_BLOCK)
    s_pad = n_blk * MOBA_BLOCK
    pad = ((0, 0), (0, 0), (0, s_pad - s), (0, 0))
    q = jnp.pad(q, pad)
    k = jnp.pad(k, pad)
    v = jnp.pad(v, pad)
    k_blk = k.reshape(b, h, n_blk, MOBA_BLOCK, hd)
    v_blk = v.reshape(b, h, n_blk, MOBA_BLOCK, hd)
    k_mean = jnp.mean(k_blk.astype(jnp.float32), axis=3).astype(q.dtype)
    n_sel = min(MOBA_TOPK, n_blk)
    scale = hd ** -0.5
    n_chunks = s_pad // MOBA_Q_CHUNK
    q_chunks = jnp.moveaxis(q.reshape(b, h, n_chunks, MOBA_Q_CHUNK, hd), 2, 0)
    b_idx = jnp.arange(b)[:, None, None, None]
    h_idx = jnp.arange(h)[None, :, None, None]
    offs = jnp.arange(MOBA_BLOCK)
    blk_ids = jnp.arange(n_blk)

    def chunk(args):
        qc, ci = args
        q_start = ci * MOBA_Q_CHUNK
        q_pos = q_start + jnp.arange(MOBA_Q_CHUNK)
        own = q_start // MOBA_BLOCK
        gate = jnp.einsum('bhqd,bhnd->bhqn', qc, k_mean).astype(jnp.float32)
        gate = jnp.where(blk_ids < own, gate, NEG_INF)
        _, sel = lax.top_k(gate, n_sel)
        sel_ok = sel < own
        k_sel = k_blk[b_idx, h_idx, sel]
        v_sel = v_blk[b_idx, h_idx, sel]
        s_sel = jnp.einsum('bhqd,bhqnkd->bhqnk', qc, k_sel).astype(jnp.float32) * scale
        rel_sel = q_pos[:, None, None] - (sel[..., None] * MOBA_BLOCK + offs)
        bias_sel = rel_table[h_idx[..., None], t5_bucket(rel_sel)]
        s_sel = jnp.where(sel_ok[..., None], s_sel + bias_sel, NEG_INF)
        k_own = lax.dynamic_slice_in_dim(k, own * MOBA_BLOCK, MOBA_BLOCK, axis=2)
        v_own = lax.dynamic_slice_in_dim(v, own * MOBA_BLOCK, MOBA_BLOCK, axis=2)
        s_own = jnp.einsum('bhqd,bhkd->bhqk', qc, k_own).astype(jnp.float32) * scale
        rel_own = q_pos[:, None] - (own * MOBA_BLOCK + offs)[None, :]
        bias_own = rel_table[:, t5_bucket(rel_own)]
        s_own = jnp.where(rel_own >= 0, s_own + bias_own, NEG_INF)
        logits = jnp.concatenate(
            [s_sel.reshape(b, h, MOBA_Q_CHUNK, n_sel * MOBA_BLOCK), s_own], axis=-1)
        p = jax.nn.softmax(logits, axis=-1).astype(v.dtype)
        p_sel = p[..., :n_sel * MOBA_BLOCK].reshape(b, h, MOBA_Q_CHUNK, n_sel, MOBA_BLOCK)
        p_own = p[..., n_sel * MOBA_BLOCK:]
        return (jnp.einsum('bhqnk,bhqnkd->bhqd', p_sel, v_sel)
                + jnp.einsum('bhqk,bhkd->bhqd', p_own, v_own))

    out = lax.map(chunk, (q_chunks, jnp.arange(n_chunks)))
    return jnp.moveaxis(out, 0, 2).reshape(b, h, s_pad, hd)[:, :, :s]


def stick_breaking_attention(q, k, v):
    b, h, s, hd = q.shape
    n_qb = s // SB_Q_BLOCK
    scale = hd ** -0.5
    k_pos = jnp.arange(s)
    q_blocks = jnp.moveaxis(q.reshape(b, h, n_qb, SB_Q_BLOCK, hd), 2, 0)

    def block(args):
        qb, bi = args
        q_pos = bi * SB_Q_BLOCK + jnp.arange(SB_Q_BLOCK)
        z = jnp.einsum('bhqd,bhkd->bhqk', qb, k).astype(jnp.float32) * scale
        past = k_pos[None, :] < q_pos[:, None]
        log_beta = jax.nn.log_sigmoid(z)
        log_keep = jnp.where(past, jax.nn.log_sigmoid(-z), 0.0)
        log_later = lax.cumsum(log_keep, axis=3, reverse=True) - log_keep
        a = jnp.where(past, jnp.exp(log_beta + log_later), 0.0)
        return jnp.einsum('bhqk,bhkd->bhqd', a.astype(v.dtype), v)

    out = lax.map(block, (q_blocks, jnp.arange(n_qb)))
    return jnp.moveaxis(out, 0, 2).reshape(b, h, s, hd)


def causal_depthwise_conv(u, w, bias):
    ch = u.shape[-1]
    out = lax.conv_general_dilated(
        u, w[:, None, :], window_strides=(1,), padding=[(CONV_WIDTH - 1, 0)],
        dimension_numbers=('NWC', 'WIO', 'NWC'), feature_group_count=ch)
    return out + bias


def setup_inputs(seed: int = 0) -> dict:
    key = jax.random.key(seed)
    ks = jax.random.split(key, 16)
    f32 = jnp.float32
    nrm = lambda k, shp, sc: jax.random.normal(k, shp, f32) * sc
    return {
        'x': nrm(ks[0], (BATCH, SEQ, D_MODEL), 1.0),
        'c': nrm(ks[1], (BATCH, D_MODEL), 1.0),
        'w_ada': nrm(ks[2], (DEPTH, D_MODEL, 6 * D_MODEL), D_MODEL ** -0.5),
        'b_ada': nrm(ks[3], (DEPTH, 6 * D_MODEL), 0.02),
        'g_mix': 1.0 + nrm(ks[4], (DEPTH, D_MODEL), 0.05),
        'w_in': nrm(ks[5], (DEPTH, D_MODEL, IN_WIDTH), D_MODEL ** -0.5),
        'w_br_moba': nrm(ks[6], (DEPTH, MOBA_WIDTH, D_MODEL), MOBA_WIDTH ** -0.5),
        'w_br_sb': nrm(ks[7], (DEPTH, SB_WIDTH, D_MODEL), SB_WIDTH ** -0.5),
        'w_out': nrm(ks[8], (DEPTH, D_MODEL, D_MODEL), D_MODEL ** -0.5),
        'rel_bias': nrm(ks[9], (N_HEADS_MOBA, REL_BUCKETS), 0.5),
        'g_ffn': 1.0 + nrm(ks[10], (DEPTH, D_MODEL), 0.05),
        'w_up': nrm(ks[11], (DEPTH, D_MODEL, 2 * D_FF), D_MODEL ** -0.5),
        'w_conv': nrm(ks[12], (DEPTH, CONV_WIDTH, 2 * D_FF), CONV_WIDTH ** -0.5),
        'b_conv': nrm(ks[13], (DEPTH, 2 * D_FF), 0.02),
        'w_down': nrm(ks[14], (DEPTH, D_FF, D_MODEL), D_FF ** -0.5),
        'g_final': 1.0 + nrm(ks[15], (D_MODEL,), 0.05),
    }


def reference(x, c, w_ada, b_ada, g_mix, w_in, w_br_moba, w_br_sb, w_out, rel_bias,
              g_ffn, w_up, w_conv, b_conv, w_down, g_final):
    splits = list(np.cumsum([MOBA_WIDTH, MOBA_WIDTH, MOBA_WIDTH, SB_WIDTH, SB_WIDTH, SB_WIDTH]))
    c_act = jax.nn.silu(c)
    for l in range(DEPTH):
        mod = c_act @ w_ada[l] + b_ada[l]
        sh_m, sc_m, gt_m, sh_f, sc_f, gt_f = jnp.split(mod, 6, axis=-1)
        h = modulate(rms_norm(x, g_mix[l]), sh_m, sc_m)
        proj = jnp.einsum('bsd,de->bse', h, w_in[l])
        qa, ka, va, qb, kb, vb, gate_logits = jnp.split(proj, splits, axis=-1)
        oa = moba_attention(to_heads(qa, N_HEADS_MOBA), to_heads(ka, N_HEADS_MOBA),
                            to_heads(va, N_HEADS_MOBA), rel_bias)
        ob = stick_breaking_attention(to_heads(qb, N_HEADS_SB), to_heads(kb, N_HEADS_SB),
                                      to_heads(vb, N_HEADS_SB))
        ya = jnp.einsum('bse,ed->bsd', merge_heads(oa), w_br_moba[l])
        yb = jnp.einsum('bse,ed->bsd', merge_heads(ob), w_br_sb[l])
        gates = jax.nn.sigmoid(gate_logits.astype(jnp.float32)).astype(x.dtype)
        g_a, g_b = jnp.split(gates, 2, axis=-1)
        mixed = jnp.einsum('bsd,de->bse', g_a * ya + g_b * yb, w_out[l])
        x = x + gt_m[:, None, :] * mixed
        h2 = modulate(rms_norm(x, g_ffn[l]), sh_f, sc_f)
        u = jnp.einsum('bsd,df->bsf', h2, w_up[l])
        u = causal_depthwise_conv(u, w_conv[l], b_conv[l])
        u_val, u_gate = jnp.split(u, 2, axis=-1)
        y = jnp.einsum('bsf,fd->bsd', jax.nn.gelu(u_gate) * u_val, w_down[l])
        x = x + gt_f[:, None, :] * y
    return rms_norm(x, g_final)
```

```python
import functools
import math

import numpy as np
import jax
import jax.numpy as jnp
from jax import lax
from jax.experimental import pallas as pl
from jax.experimental.pallas import tpu as pltpu

D_MODEL = 1024
HEAD_DIM = 64
N_HEADS = 8
MIX_WIDTH = N_HEADS * HEAD_DIM
IN_WIDTH = 6 * MIX_WIDTH + 2 * D_MODEL
MOBA_BLOCK = 256
MOBA_TOPK = 3
D_FF = 2816
REL_BUCKETS = 32
REL_MAX_DIST = 128
NORM_EPS = 1e-6
NEG_INF = -1e30

LANES = 128
PAIR = LANES // HEAD_DIM
N_PAIRS = N_HEADS // PAIR
BLK = MOBA_BLOCK
VMEM_LIMIT = 56 * 1024 * 1024

F32 = jnp.float32
BF16 = jnp.bfloat16


def _dot(a, b):
    return jnp.dot(a, b, preferred_element_type=F32)


def _dot_nt(a, b):
    return lax.dot_general(a, b, (((1,), (1,)), ((), ())), preferred_element_type=F32)


def _rms(x):
    return x * lax.rsqrt(jnp.mean(x * x, axis=-1, keepdims=True) + NORM_EPS)


def _ada_kernel(c_ref, w_ref, b_ref, o_ref):
    c = c_ref[...]
    ca = c * jax.nn.sigmoid(c)
    o_ref[...] = jnp.dot(ca, w_ref[...], precision=lax.Precision.HIGHEST,
                         preferred_element_type=F32) + b_ref[...]


def _ada(c, w_ada, b_ada):
    b, d = c.shape
    n = w_ada.shape[1]
    tn = D_MODEL
    return pl.pallas_call(
        _ada_kernel,
        grid=(n // tn,),
        in_specs=[pl.BlockSpec((b, d), lambda j: (0, 0)),
                  pl.BlockSpec((d, tn), lambda j: (0, j)),
                  pl.BlockSpec((1, tn), lambda j: (0, j))],
        out_specs=pl.BlockSpec((b, tn), lambda j: (0, j)),
        out_shape=jax.ShapeDtypeStruct((b, n), F32),
        compiler_params=pltpu.CompilerParams(dimension_semantics=("arbitrary",),
                                             vmem_limit_bytes=VMEM_LIMIT),
        name="ada",
    )(c, w_ada, b_ada.reshape(1, n))


def _bucket_tiles():
    def bucket(rel):
        n = np.maximum(rel, 0)
        max_exact = REL_BUCKETS // 2
        n_f = np.maximum(n, max_exact).astype(np.float32)
        large = max_exact + (np.log(n_f / np.float32(max_exact)) / np.float32(math.log(REL_MAX_DIST / max_exact))
                             * np.float32(REL_BUCKETS - max_exact)).astype(np.int32)
        large = np.minimum(large, REL_BUCKETS - 1)
        return np.where(n < max_exact, n, large).astype(np.int32)
    qi = np.arange(BLK)[:, None]
    kj = np.arange(BLK)[None, :]
    t0 = np.where(qi >= kj, bucket(qi - kj), REL_BUCKETS)
    t1 = bucket(qi - kj + BLK)
    far = bucket(np.array([2 * BLK - (BLK - 1)]))
    assert int(far[0]) == REL_BUCKETS - 1
    return np.stack([t0, t1]).astype(np.int32)


def _bias_kernel(tab_ref, bkt_ref, o_ref):
    h = pl.program_id(0)
    for t in range(2):
        b = bkt_ref[t]
        acc = jnp.full((BLK, BLK), NEG_INF, F32)
        for k in range(REL_BUCKETS):
            acc = jnp.where(b == k, tab_ref[h, k], acc)
        o_ref[0, t] = acc
    o_ref[0, 2] = jnp.full((BLK, BLK), tab_ref[h, REL_BUCKETS - 1], F32)


def _bias_tiles(rel_bias):
    bkt = jnp.asarray(_bucket_tiles())
    return pl.pallas_call(
        _bias_kernel,
        grid=(N_HEADS,),
        in_specs=[pl.BlockSpec(memory_space=pltpu.SMEM),
                  pl.BlockSpec((2, BLK, BLK), lambda h: (0, 0, 0))],
        out_specs=pl.BlockSpec((1, 3, BLK, BLK), lambda h: (h, 0, 0, 0)),
        out_shape=jax.ShapeDtypeStruct((N_HEADS, 3, BLK, BLK), F32),
        compiler_params=pltpu.CompilerParams(dimension_semantics=("arbitrary",)),
        name="bias_tiles",
    )(rel_bias, bkt)


PROJ_TS = 256
PROJ_NC = 512


def _proj_kernel(x_ref, mod_ref, g_ref, w_ref, o_ref):
    x = x_ref[0]
    sh = mod_ref[0, 0:1, :]
    sc = mod_ref[0, 1:2, :]
    h = (_rms(x) * g_ref[...] * (1.0 + sc) + sh).astype(BF16)
    scale = HEAD_DIM ** -0.5
    for n0 in range(0, IN_WIDTH, PROJ_NC):
        r = _dot(h, w_ref[:, n0:n0 + PROJ_NC])
        if n0 in (0, 3 * MIX_WIDTH):
            r = r * scale
        elif n0 >= 6 * MIX_WIDTH:
            r = jax.nn.sigmoid(r)
        o_ref[0, :, n0:n0 + PROJ_NC] = r.astype(BF16)


def _proj(x, mod, g_mix, w_in_bf16):
    b, s, d = x.shape
    ts = PROJ_TS
    return pl.pallas_call(
        _proj_kernel,
        grid=(b, s // ts),
        in_specs=[pl.BlockSpec((1, ts, d), lambda i, j: (i, j, 0)),
                  pl.BlockSpec((1, 6, d), lambda i, j: (i, 0, 0)),
                  pl.BlockSpec((1, d), lambda i, j: (0, 0)),
                  pl.BlockSpec((d, IN_WIDTH), lambda i, j: (0, 0), pipeline_mode=pl.Buffered(1))],
        out_specs=pl.BlockSpec((1, ts, IN_WIDTH), lambda i, j: (i, j, 0)),
        out_shape=jax.ShapeDtypeStruct((b, s, IN_WIDTH), BF16),
        compiler_params=pltpu.CompilerParams(dimension_semantics=("arbitrary", "arbitrary"),
                                             vmem_limit_bytes=VMEM_LIMIT),
        name="proj",
    )(x, mod, g_mix.reshape(1, d), w_in_bf16)


def _head_lanes(shape, h):
    lane = lax.broadcasted_iota(jnp.int32, shape, len(shape) - 1)
    return (lane >= h * HEAD_DIM) & (lane < (h + 1) * HEAD_DIM)


def _moba_kernel(q_ref, k_ref, v_ref, bias_ref, o_ref, kaug_ref, km_ref, pen_ref):
    i = pl.program_id(2)
    seq = k_ref.shape[1]
    n_blk = seq // BLK

    @pl.when(i == 0)
    def _prepare():
        k = k_ref[0]
        r = lax.broadcasted_iota(jnp.int32, (LANES, seq), 0)
        c = lax.broadcasted_iota(jnp.int32, (LANES, seq), 1)
        avg = jnp.where(c // BLK == r, 1.0 / BLK, 0.0).astype(BF16)
        kmean = _dot(avg, k)
        row = lax.broadcasted_iota(jnp.int32, (seq, LANES), 0)
        lane = lax.broadcasted_iota(jnp.int32, (seq, LANES), 1)
        for h in range(PAIR):
            ind0 = (1 - h) * HEAD_DIM
            ind = (lane - ind0) == (row // BLK)
            kaug_ref[h] = jnp.where(_head_lanes((seq, LANES), h), k, jnp.where(ind, 1.0, 0.0).astype(BF16))
            kmh = jnp.where(_head_lanes((LANES, LANES), h), kmean, 0.0)
            hi = kmh.astype(BF16)
            km_ref[h, 0] = hi
            km_ref[h, 1] = (kmh - hi.astype(F32)).astype(BF16)

    q = q_ref[0]
    rr = lax.broadcasted_iota(jnp.int32, (BLK, BLK), 0)
    cc = lax.broadcasted_iota(jnp.int32, (BLK, BLK), 1)
    eye = jnp.where(rr == cc, 1.0, 0.0).astype(BF16)
    jidx = lax.broadcasted_iota(jnp.int32, (8, BLK), 0)
    outs = []
    for h in range(PAIR):
        g_t = _dot_nt(km_ref[h, 0], q) + _dot_nt(km_ref[h, 1], q)
        g8 = g_t[0:8, :]
        cnt = jnp.zeros((8, BLK), jnp.int32)
        for jp in range(n_blk):
            rowv = g8[jp:jp + 1, :]
            beats = (rowv > g8) | ((rowv == g8) & (jp < jidx))
            cnt = cnt + jnp.where(beats & (jp < i), 1, 0)
        sel = (jidx == i) | ((jidx < i) & (cnt < MOBA_TOPK))
        pen_t = jnp.where(sel, 0.0, NEG_INF)
        r0 = (1 - h) * HEAD_DIM
        pen_ref[...] = jnp.zeros((LANES, BLK), F32)
        pen_ref[r0:r0 + 8, :] = pen_t
        pen_cols = _dot_nt(eye, pen_ref[...].astype(BF16))
        qa = jnp.where(_head_lanes((BLK, LANES), h), q, pen_cols.astype(BF16))

        def body(jj, carry):
            m, l, acc = carry
            j = i - jj
            off = pl.multiple_of(j * BLK, BLK)
            kj = kaug_ref[h, pl.ds(off, BLK), :]
            s = _dot_nt(qa, kj) + bias_ref[h, jnp.minimum(jj, 2)]
            m_new = jnp.maximum(m, jnp.max(s, axis=-1, keepdims=True))
            alpha = jnp.exp(m - m_new)
            p = jnp.exp(s - m_new)
            l_new = alpha * l + jnp.sum(p, axis=-1, keepdims=True)
            vj = v_ref[0, pl.ds(off, BLK), :]
            acc_new = alpha * acc + _dot(p.astype(BF16), vj)
            return m_new, l_new, acc_new

        init = (jnp.full((BLK, 1), NEG_INF, F32), jnp.zeros((BLK, 1), F32), jnp.zeros((BLK, LANES), F32))
        _, l, acc = lax.fori_loop(0, i + 1, body, init)
        outs.append(acc / l)
    o_ref[0] = jnp.where(_head_lanes((BLK, LANES), 0), outs[0], outs[1]).astype(o_ref.dtype)


def _moba(proj, bias):
    b, s, _ = proj.shape
    nq = s // BLK
    q_off, k_off, v_off = 0, MIX_WIDTH // LANES, 2 * MIX_WIDTH // LANES
    return pl.pallas_call(
        _moba_kernel,
        grid=(b, N_PAIRS, nq),
        in_specs=[pl.BlockSpec((1, BLK, LANES), lambda bi, p, i: (bi, i, q_off + p)),
                  pl.BlockSpec((1, s, LANES), lambda bi, p, i: (bi, 0, k_off + p)),
                  pl.BlockSpec((1, s, LANES), lambda bi, p, i: (bi, 0, v_off + p)),
                  pl.BlockSpec((PAIR, 3, BLK, BLK), lambda bi, p, i: (p, 0, 0, 0))],
        out_specs=pl.BlockSpec((1, BLK, LANES), lambda bi, p, i: (bi, i, p)),
        out_shape=jax.ShapeDtypeStruct((b, s, MIX_WIDTH), BF16),
        scratch_shapes=[pltpu.VMEM((PAIR, s, LANES), BF16),
                        pltpu.VMEM((PAIR, 2, LANES, LANES), BF16),
                        pltpu.VMEM((LANES, BLK), F32)],
        compiler_params=pltpu.CompilerParams(dimension_semantics=("arbitrary",) * 3,
                                             vmem_limit_bytes=VMEM_LIMIT),
        name="moba",
    )(proj, proj, proj, bias)


def _sb_kernel(q_ref, k_ref, v_ref, o_ref):
    i = pl.program_id(2)
    q = q_ref[0]
    rr = lax.broadcasted_iota(jnp.int32, (BLK, BLK), 0)
    cc = lax.broadcasted_iota(jnp.int32, (BLK, BLK), 1)
    upper = jnp.where(rr > cc, 1.0, 0.0).astype(BF16)
    past = cc < rr
    outs = []
    for h in range(PAIR):
        qh = jnp.where(_head_lanes((BLK, LANES), h), q, jnp.zeros_like(q))

        def tile(j, carry, acc, diag):
            off = pl.multiple_of(j * BLK, BLK)
            z = _dot_nt(qh, k_ref[0, pl.ds(off, BLK), :])
            sp = jnp.log(1.0 + jnp.exp(-jnp.abs(z)))
            log_beta = jnp.minimum(z, 0.0) - sp
            log_keep = log_beta - z
            if diag:
                log_keep = jnp.where(past, log_keep, 0.0)
            hi = log_keep.astype(BF16)
            lo = (log_keep - hi.astype(F32)).astype(BF16)
            within = _dot(hi, upper) + _dot(lo, upper)
            a = jnp.exp(log_beta + carry + within)
            if diag:
                a = jnp.where(past, a, 0.0)
            acc = acc + _dot(a.astype(BF16), v_ref[0, pl.ds(off, BLK), :])
            carry = carry + jnp.sum(log_keep, axis=-1, keepdims=True)
            return carry, acc

        carry, acc = tile(i, jnp.zeros((BLK, 1), F32), jnp.zeros((BLK, LANES), F32), True)

        def body(jj, c):
            return tile(i - 1 - jj, c[0], c[1], False)

        carry, acc = lax.fori_loop(0, i, body, (carry, acc))
        outs.append(acc)
    o_ref[0] = jnp.where(_head_lanes((BLK, LANES), 0), outs[0], outs[1]).astype(o_ref.dtype)


def _sb(proj):
    b, s, _ = proj.shape
    nq = s // BLK
    base = 3 * MIX_WIDTH // LANES
    q_off, k_off, v_off = base, base + MIX_WIDTH // LANES, base + 2 * MIX_WIDTH // LANES
    return pl.pallas_call(
        _sb_kernel,
        grid=(b, N_PAIRS, nq),
        in_specs=[pl.BlockSpec((1, BLK, LANES), lambda bi, p, i: (bi, i, q_off + p)),
                  pl.BlockSpec((1, s, LANES), lambda bi, p, i: (bi, 0, k_off + p)),
                  pl.BlockSpec((1, s, LANES), lambda bi, p, i: (bi, 0, v_off + p))],
        out_specs=pl.BlockSpec((1, BLK, LANES), lambda bi, p, i: (bi, i, p)),
        out_shape=jax.ShapeDtypeStruct((b, s, MIX_WIDTH), BF16),
        compiler_params=pltpu.CompilerParams(dimension_semantics=("arbitrary",) * 3,
                                             vmem_limit_bytes=VMEM_LIMIT),
        name="sb",
    )(proj, proj, proj)


POST_TS = 256
FF_CHUNK = D_FF // 2
HALO = 8


def _gelu_tanh(x):
    return 0.5 * x * (1.0 + jnp.tanh(math.sqrt(2.0 / math.pi) * (x + 0.044715 * (x * x * x))))


def _post_kernel(x_ref, oa_ref, ob_ref, ga_ref, gb_ref, mod_ref, gffn_ref, gfin_ref,
                 wbm_ref, wbs_ref, wout_ref, wup_ref, wconv_ref, bconv_ref, wdown_ref,
                 o_ref, halo_ref, ubuf_ref):
    ts = x_ref.shape[1]

    @pl.when(pl.program_id(1) == 0)
    def _zero_halo():
        halo_ref[...] = jnp.zeros_like(halo_ref)

    gt_m = mod_ref[0, 2:3, :]
    sh_f = mod_ref[0, 3:4, :]
    sc_f = mod_ref[0, 4:5, :]
    gt_f = mod_ref[0, 5:6, :]

    ya = _dot(oa_ref[0], wbm_ref[...])
    yb = _dot(ob_ref[0], wbs_ref[...])
    mix = (ga_ref[0].astype(F32) * ya + gb_ref[0].astype(F32) * yb).astype(BF16)
    x1 = x_ref[0] + gt_m * _dot(mix, wout_ref[...])
    h2 = (_rms(x1) * gffn_ref[...] * (1.0 + sc_f) + sh_f).astype(BF16)

    def conv(u, c0):
        cols = slice(c0, c0 + FF_CHUNK)
        ubuf_ref[0:HALO, :] = halo_ref[:, cols]
        ubuf_ref[HALO:HALO + ts, :] = u
        halo_ref[:, cols] = u[ts - HALO:ts, :]
        p1 = ubuf_ref[HALO - 1:HALO - 1 + ts, :]
        p2 = ubuf_ref[HALO - 2:HALO - 2 + ts, :]
        w = wconv_ref[:, cols]
        return w[0:1, :] * p2 + w[1:2, :] * p1 + w[2:3, :] * u + bconv_ref[:, cols]

    y = jnp.zeros((ts, D_MODEL), F32)
    for c0 in range(0, D_FF, FF_CHUNK):
        u_val = conv(_dot(h2, wup_ref[:, c0:c0 + FF_CHUNK]), c0)
        u_gate = conv(_dot(h2, wup_ref[:, D_FF + c0:D_FF + c0 + FF_CHUNK]), D_FF + c0)
        act = (_gelu_tanh(u_gate) * u_val).astype(BF16)
        y = y + _dot(act, wdown_ref[c0:c0 + FF_CHUNK, :])
    x2 = x1 + gt_f * y
    o_ref[0] = _rms(x2) * gfin_ref[...]


def _post(x, oa, ob, proj, mod, g_ffn, g_final, wbm, wbs, wout, wup, wconv, bconv, wdown):
    b, s, d = x.shape
    ts = POST_TS
    ga_blk = 6 * MIX_WIDTH // D_MODEL
    tok = lambda i, j: (i, j, 0)
    const2 = lambda i, j: (0, 0)
    resident = functools.partial(pl.BlockSpec, index_map=const2, pipeline_mode=pl.Buffered(1))
    return pl.pallas_call(
        _post_kernel,
        grid=(b, s // ts),
        in_specs=[pl.BlockSpec((1, ts, d), tok),
                  pl.BlockSpec((1, ts, MIX_WIDTH), tok),
                  pl.BlockSpec((1, ts, MIX_WIDTH), tok),
                  pl.BlockSpec((1, ts, d), lambda i, j: (i, j, ga_blk)),
                  pl.BlockSpec((1, ts, d), lambda i, j: (i, j, ga_blk + 1)),
                  pl.BlockSpec((1, 6, d), lambda i, j: (i, 0, 0)),
                  pl.BlockSpec((1, d), const2),
                  pl.BlockSpec((1, d), const2),
                  resident((MIX_WIDTH, d)),
                  resident((MIX_WIDTH, d)),
                  resident((d, d)),
                  resident((d, 2 * D_FF)),
                  pl.BlockSpec((3, 2 * D_FF), const2),
                  pl.BlockSpec((1, 2 * D_FF), const2),
                  resident((D_FF, d))],
        out_specs=pl.BlockSpec((1, ts, d), tok),
        out_shape=jax.ShapeDtypeStruct((b, s, d), x.dtype),
        scratch_shapes=[pltpu.VMEM((HALO, 2 * D_FF), F32),
                        pltpu.VMEM((HALO + ts, FF_CHUNK), F32)],
        compiler_params=pltpu.CompilerParams(dimension_semantics=("arbitrary", "arbitrary"),
                                             vmem_limit_bytes=VMEM_LIMIT),
        name="post",
    )(x, oa, ob, proj, proj, mod, g_ffn.reshape(1, d), g_final.reshape(1, d),
      wbm, wbs, wout, wup, wconv, bconv.reshape(1, 2 * D_FF), wdown)


@jax.jit
def kernel(x, c, w_ada, b_ada, g_mix, w_in, w_br_moba, w_br_sb, w_out, rel_bias, g_ffn, w_up,
           w_conv, b_conv, w_down, g_final):
    assert w_ada.shape[0] == 1, "the final rms_norm is fused into the single layer's last kernel"
    l = 0
    bias = _bias_tiles(rel_bias)
    mod = _ada(c, w_ada[l], b_ada[l]).reshape(x.shape[0], 6, D_MODEL)
    proj = _proj(x, mod, g_mix[l], w_in[l].astype(BF16))
    oa = _moba(proj, bias)
    ob = _sb(proj)
    return _post(x, oa, ob, proj, mod, g_ffn[l], g_final,
                 w_br_moba[l].astype(BF16), w_br_sb[l].astype(BF16), w_out[l].astype(BF16),
                 w_up[l].astype(BF16), w_conv[l], b_conv[l], w_down[l].astype(BF16))
```

```python
import functools
import math

import numpy as np
import jax
import jax.numpy as jnp
from jax import lax
from jax.experimental import pallas as pl
from jax.experimental.pallas import tpu as pltpu

D_MODEL = 1024
HEAD_DIM = 64
N_HEADS = 8
MIX_WIDTH = N_HEADS * HEAD_DIM
IN_WIDTH = 6 * MIX_WIDTH + 2 * D_MODEL
MOBA_BLOCK = 256
MOBA_TOPK = 3
D_FF = 2816
REL_BUCKETS = 32
REL_MAX_DIST = 128
NORM_EPS = 1e-6
NEG_INF = -1e30

LANES = 128
PAIR = LANES // HEAD_DIM
N_PAIRS = N_HEADS // PAIR
BLK = MOBA_BLOCK
VMEM_LIMIT = 56 * 1024 * 1024

F32 = jnp.float32
BF16 = jnp.bfloat16


def _dot(a, b):
    return jnp.dot(a, b, preferred_element_type=F32)


def _dot_nt(a, b):
    return lax.dot_general(a, b, (((1,), (1,)), ((), ())), preferred_element_type=F32)


def _rms(x):
    return x * lax.rsqrt(jnp.mean(x * x, axis=-1, keepdims=True) + NORM_EPS)


def _ada_kernel(c_ref, w_ref, b_ref, o_ref):
    c = c_ref[...]
    ca = c * jax.nn.sigmoid(c)
    o_ref[...] = jnp.dot(ca, w_ref[...], precision=lax.Precision.HIGHEST,
                         preferred_element_type=F32) + b_ref[...]


def _ada(c, w_ada, b_ada):
    b, d = c.shape
    n = w_ada.shape[1]
    tn = D_MODEL
    return pl.pallas_call(
        _ada_kernel,
        grid=(n // tn,),
        in_specs=[pl.BlockSpec((b, d), lambda j: (0, 0)),
                  pl.BlockSpec((d, tn), lambda j: (0, j)),
                  pl.BlockSpec((1, tn), lambda j: (0, j))],
        out_specs=pl.BlockSpec((b, tn), lambda j: (0, j)),
        out_shape=jax.ShapeDtypeStruct((b, n), F32),
        compiler_params=pltpu.CompilerParams(dimension_semantics=("arbitrary",),
                                             vmem_limit_bytes=VMEM_LIMIT),
        name="ada",
    )(c, w_ada, b_ada.reshape(1, n))


def _bucket_tiles():
    def bucket(rel):
        n = np.maximum(rel, 0)
        max_exact = REL_BUCKETS // 2
        n_f = np.maximum(n, max_exact).astype(np.float32)
        large = max_exact + (np.log(n_f / np.float32(max_exact)) / np.float32(math.log(REL_MAX_DIST / max_exact))
                             * np.float32(REL_BUCKETS - max_exact)).astype(np.int32)
        large = np.minimum(large, REL_BUCKETS - 1)
        return np.where(n < max_exact, n, large).astype(np.int32)
    qi = np.arange(BLK)[:, None]
    kj = np.arange(BLK)[None, :]
    t0 = np.where(qi >= kj, bucket(qi - kj), REL_BUCKETS)
    t1 = bucket(qi - kj + BLK)
    far = bucket(np.array([2 * BLK - (BLK - 1)]))
    assert int(far[0]) == REL_BUCKETS - 1
    return np.stack([t0, t1]).astype(np.int32)


def _bias_kernel(tab_ref, bkt_ref, o_ref):
    h = pl.program_id(0)
    for t in range(2):
        b = bkt_ref[t]
        acc = jnp.full((BLK, BLK), NEG_INF, F32)
        for k in range(REL_BUCKETS):
            acc = jnp.where(b == k, tab_ref[h, k], acc)
        o_ref[0, t] = acc
    o_ref[0, 2] = jnp.full((BLK, BLK), tab_ref[h, REL_BUCKETS - 1], F32)


def _bias_tiles(rel_bias):
    bkt = jnp.asarray(_bucket_tiles())
    return pl.pallas_call(
        _bias_kernel,
        grid=(N_HEADS,),
        in_specs=[pl.BlockSpec(memory_space=pltpu.SMEM),
                  pl.BlockSpec((2, BLK, BLK), lambda h: (0, 0, 0))],
        out_specs=pl.BlockSpec((1, 3, BLK, BLK), lambda h: (h, 0, 0, 0)),
        out_shape=jax.ShapeDtypeStruct((N_HEADS, 3, BLK, BLK), F32),
        compiler_params=pltpu.CompilerParams(dimension_semantics=("arbitrary",)),
        name="bias_tiles",
    )(rel_bias, bkt)


PROJ_TS = 256
PROJ_NC = 512


def _proj_kernel(x_ref, mod_ref, g_ref, w_ref, o_ref):
    x = x_ref[0]
    sh = mod_ref[0, 0:1, :]
    sc = mod_ref[0, 1:2, :]
    h = (_rms(x) * g_ref[...] * (1.0 + sc) + sh).astype(BF16)
    scale = HEAD_DIM ** -0.5
    for n0 in range(0, IN_WIDTH, PROJ_NC):
        r = _dot(h, w_ref[:, n0:n0 + PROJ_NC])
        if n0 in (0, 3 * MIX_WIDTH):
            r = r * scale
        elif n0 >= 6 * MIX_WIDTH:
            r = jax.nn.sigmoid(r)
        o_ref[0, :, n0:n0 + PROJ_NC] = r.astype(BF16)


def _proj(x, mod, g_mix, w_in_bf16):
    b, s, d = x.shape
    ts = PROJ_TS
    return pl.pallas_call(
        _proj_kernel,
        grid=(b, s // ts),
        in_specs=[pl.BlockSpec((1, ts, d), lambda i, j: (i, j, 0)),
                  pl.BlockSpec((1, 6, d), lambda i, j: (i, 0, 0)),
                  pl.BlockSpec((1, d), lambda i, j: (0, 0)),
                  pl.BlockSpec((d, IN_WIDTH), lambda i, j: (0, 0), pipeline_mode=pl.Buffered(1))],
        out_specs=pl.BlockSpec((1, ts, IN_WIDTH), lambda i, j: (i, j, 0)),
        out_shape=jax.ShapeDtypeStruct((b, s, IN_WIDTH), BF16),
        compiler_params=pltpu.CompilerParams(dimension_semantics=("arbitrary", "arbitrary"),
                                             vmem_limit_bytes=VMEM_LIMIT),
        name="proj",
    )(x, mod, g_mix.reshape(1, d), w_in_bf16)


def _head_lanes(shape, h):
    lane = lax.broadcasted_iota(jnp.int32, shape, len(shape) - 1)
    return (lane >= h * HEAD_DIM) & (lane < (h + 1) * HEAD_DIM)


def _rows(blk):
    return slice(blk * BLK, (blk + 1) * BLK)


def _run_chains(chains, step):
    states = [None] * len(chains)
    for k in range(max(len(c) for c in chains)):
        for ci, chain in enumerate(chains):
            if k < len(chain):
                states[ci] = step(chain[k], states[ci])
    return states


def _pair_chains(t, n_blk):
    qis = (t, n_blk - 1 - t)
    return qis, [[(qi, j, h) for j in range(qi, -1, -1)] for qi in qis for h in range(PAIR)]


def _store_pair(o_ref, qi, out0, out1):
    o_ref[0, _rows(qi), :] = jnp.where(_head_lanes((BLK, LANES), 0), out0, out1).astype(o_ref.dtype)


def _moba_prepare(q_ref, k_ref, qaug_ref, kaug_ref, pen_ref):
    seq = k_ref.shape[1]
    n_blk = seq // BLK
    k = k_ref[0]
    q = q_ref[0]
    sub = 16
    r = lax.broadcasted_iota(jnp.int32, (sub, seq), 0)
    c = lax.broadcasted_iota(jnp.int32, (sub, seq), 1)
    avg = jnp.where(c // BLK == r, 1.0 / BLK, 0.0).astype(BF16)
    kmean = _dot(avg, k)
    row = lax.broadcasted_iota(jnp.int32, (seq, LANES), 0)
    lane = lax.broadcasted_iota(jnp.int32, (seq, LANES), 1)
    rr = lax.broadcasted_iota(jnp.int32, (BLK, BLK), 0)
    cc = lax.broadcasted_iota(jnp.int32, (BLK, BLK), 1)
    eye = jnp.where(rr == cc, 1.0, 0.0).astype(BF16)
    jidx = lax.broadcasted_iota(jnp.int32, (8, seq), 0)
    qblk = lax.broadcasted_iota(jnp.int32, (8, seq), 1) // BLK
    for h in range(PAIR):
        ind0 = (1 - h) * HEAD_DIM
        ind = (lane - ind0) == (row // BLK)
        kaug_ref[h] = jnp.where(_head_lanes((seq, LANES), h), k, jnp.where(ind, 1.0, 0.0).astype(BF16))
        kmh = jnp.where(_head_lanes((sub, LANES), h), kmean, 0.0)
        hi = kmh.astype(BF16)
        lo = (kmh - hi.astype(F32)).astype(BF16)
        g8 = (_dot_nt(hi, q) + _dot_nt(lo, q))[0:8, :]
        cnt = jnp.zeros((8, seq), jnp.int32)
        for jp in range(n_blk):
            rowv = g8[jp:jp + 1, :]
            beats = (rowv > g8) | ((rowv == g8) & (jp < jidx))
            cnt = cnt + jnp.where(beats & (jp < qblk), 1, 0)
        sel = (jidx == qblk) | ((jidx < qblk) & (cnt < MOBA_TOPK))
        pen_ref[...] = jnp.zeros(pen_ref.shape, F32)
        pen_ref[ind0:ind0 + 8, :] = jnp.where(sel, 0.0, NEG_INF)
        for b in range(n_blk):
            pen_cols = _dot_nt(eye, pen_ref[:, _rows(b)].astype(BF16))
            qaug_ref[h, _rows(b), :] = jnp.where(_head_lanes((BLK, LANES), h), q[_rows(b), :],
                                                 pen_cols.astype(BF16))


def _moba_pair(t, n_blk, v_ref, bias_ref, o_ref, qaug_ref, kaug_ref):
    def step(args, st):
        qi, j, h = args
        s = _dot_nt(qaug_ref[h, _rows(qi), :], kaug_ref[h, _rows(j), :]) + bias_ref[h, min(qi - j, 2)]
        m_blk = jnp.max(s, axis=-1, keepdims=True)
        vj = v_ref[0, _rows(j), :]
        if st is None:
            p = jnp.exp(s - m_blk)
            return m_blk, jnp.sum(p, axis=-1, keepdims=True), _dot(p.astype(BF16), vj)
        m, l, acc = st
        m_new = jnp.maximum(m, m_blk)
        alpha = jnp.exp(m - m_new)
        p = jnp.exp(s - m_new)
        return (m_new, alpha * l + jnp.sum(p, axis=-1, keepdims=True),
                alpha * acc + _dot(p.astype(BF16), vj))

    qis, chains = _pair_chains(t, n_blk)
    states = _run_chains(chains, step)
    for n, qi in enumerate(qis):
        (_, l0, a0), (_, l1, a1) = states[PAIR * n], states[PAIR * n + 1]
        _store_pair(o_ref, qi, a0 / l0, a1 / l1)


def _moba_kernel(q_ref, k_ref, v_ref, bias_ref, o_ref, qaug_ref, kaug_ref, pen_ref):
    t = pl.program_id(2)
    n_blk = k_ref.shape[1] // BLK

    @pl.when(t == 0)
    def _prepare():
        _moba_prepare(q_ref, k_ref, qaug_ref, kaug_ref, pen_ref)

    for tt in range(n_blk // 2):
        pl.when(t == tt)(functools.partial(_moba_pair, tt, n_blk, v_ref, bias_ref, o_ref, qaug_ref, kaug_ref))


def _attn_specs(s, q_off, k_off, v_off):
    full = lambda off: pl.BlockSpec((1, s, LANES), lambda bi, p, t: (bi, 0, off + p))
    return [full(q_off), full(k_off), full(v_off)], pl.BlockSpec((1, s, LANES), lambda bi, p, t: (bi, 0, p))


def _moba(proj, bias):
    b, s, _ = proj.shape
    n_blk = s // BLK
    assert s % BLK == 0 and n_blk % 2 == 0 and n_blk <= 8
    in_specs, out_spec = _attn_specs(s, 0, MIX_WIDTH // LANES, 2 * MIX_WIDTH // LANES)
    return pl.pallas_call(
        _moba_kernel,
        grid=(b, N_PAIRS, n_blk // 2),
        in_specs=in_specs + [pl.BlockSpec((PAIR, 3, BLK, BLK), lambda bi, p, t: (p, 0, 0, 0))],
        out_specs=out_spec,
        out_shape=jax.ShapeDtypeStruct((b, s, MIX_WIDTH), BF16),
        scratch_shapes=[pltpu.VMEM((PAIR, s, LANES), BF16),
                        pltpu.VMEM((PAIR, s, LANES), BF16),
                        pltpu.VMEM((LANES, s), F32)],
        compiler_params=pltpu.CompilerParams(dimension_semantics=("arbitrary",) * 3,
                                             vmem_limit_bytes=VMEM_LIMIT),
        name="moba",
    )(proj, proj, proj, bias)


def _sb_pair(t, n_blk, q_ref, k_ref, v_ref, o_ref):
    rr = lax.broadcasted_iota(jnp.int32, (BLK, BLK), 0)
    cc = lax.broadcasted_iota(jnp.int32, (BLK, BLK), 1)
    upper = jnp.where(rr > cc, 1.0, 0.0).astype(BF16)
    past = cc < rr
    qis, chains = _pair_chains(t, n_blk)
    qh = {(qi, h): jnp.where(_head_lanes((BLK, LANES), h), q_ref[0, _rows(qi), :], jnp.zeros((), BF16))
          for qi in qis for h in range(PAIR)}

    def step(args, st):
        qi, j, h = args
        z = _dot_nt(qh[qi, h], k_ref[0, _rows(j), :])
        sp = jnp.log(1.0 + jnp.exp(-jnp.abs(z)))
        log_beta = jnp.minimum(z, 0.0) - sp
        log_keep = log_beta - z
        if st is None:
            log_keep = jnp.where(past, log_keep, 0.0)
        hi = log_keep.astype(BF16)
        lo = (log_keep - hi.astype(F32)).astype(BF16)
        later = _dot(hi, upper) + _dot(lo, upper)
        vj = v_ref[0, _rows(j), :]
        total = jnp.sum(log_keep, axis=-1, keepdims=True)
        if st is None:
            a = jnp.where(past, jnp.exp(log_beta + later), 0.0)
            return total, _dot(a.astype(BF16), vj)
        carry, acc = st
        a = jnp.exp(log_beta + carry + later)
        return carry + total, acc + _dot(a.astype(BF16), vj)

    states = _run_chains(chains, step)
    for n, qi in enumerate(qis):
        _store_pair(o_ref, qi, states[PAIR * n][1], states[PAIR * n + 1][1])


def _sb_kernel(q_ref, k_ref, v_ref, o_ref):
    t = pl.program_id(2)
    n_blk = k_ref.shape[1] // BLK
    for tt in range(n_blk // 2):
        pl.when(t == tt)(functools.partial(_sb_pair, tt, n_blk, q_ref, k_ref, v_ref, o_ref))


def _sb(proj):
    b, s, _ = proj.shape
    n_blk = s // BLK
    assert s % BLK == 0 and n_blk % 2 == 0
    base = 3 * MIX_WIDTH // LANES
    in_specs, out_spec = _attn_specs(s, base, base + MIX_WIDTH // LANES, base + 2 * MIX_WIDTH // LANES)
    return pl.pallas_call(
        _sb_kernel,
        grid=(b, N_PAIRS, n_blk // 2),
        in_specs=in_specs,
        out_specs=out_spec,
        out_shape=jax.ShapeDtypeStruct((b, s, MIX_WIDTH), BF16),
        compiler_params=pltpu.CompilerParams(dimension_semantics=("arbitrary",) * 3,
                                             vmem_limit_bytes=VMEM_LIMIT),
        name="sb",
    )(proj, proj, proj)


POST_TS = 256
FF_CHUNK = D_FF // 2
HALO = 8


def _gelu_tanh(x):
    return 0.5 * x * (1.0 + jnp.tanh(math.sqrt(2.0 / math.pi) * (x + 0.044715 * (x * x * x))))


def _post_kernel(x_ref, oa_ref, ob_ref, ga_ref, gb_ref, mod_ref, gffn_ref, gfin_ref,
                 wbm_ref, wbs_ref, wout_ref, wup_ref, wconv_ref, bconv_ref, wdown_ref,
                 o_ref, halo_ref, ubuf_ref):
    ts = x_ref.shape[1]

    @pl.when(pl.program_id(1) == 0)
    def _zero_halo():
        halo_ref[...] = jnp.zeros_like(halo_ref)

    gt_m = mod_ref[0, 2:3, :]
    sh_f = mod_ref[0, 3:4, :]
    sc_f = mod_ref[0, 4:5, :]
    gt_f = mod_ref[0, 5:6, :]

    ya = _dot(oa_ref[0], wbm_ref[...])
    yb = _dot(ob_ref[0], wbs_ref[...])
    mix = (ga_ref[0].astype(F32) * ya + gb_ref[0].astype(F32) * yb).astype(BF16)
    x1 = x_ref[0] + gt_m * _dot(mix, wout_ref[...])
    h2 = (_rms(x1) * gffn_ref[...] * (1.0 + sc_f) + sh_f).astype(BF16)

    def conv(u, c0):
        cols = slice(c0, c0 + FF_CHUNK)
        ubuf_ref[0:HALO, :] = halo_ref[:, cols]
        ubuf_ref[HALO:HALO + ts, :] = u
        halo_ref[:, cols] = u[ts - HALO:ts, :]
        p1 = ubuf_ref[HALO - 1:HALO - 1 + ts, :]
        p2 = ubuf_ref[HALO - 2:HALO - 2 + ts, :]
        w = wconv_ref[:, cols]
        return w[0:1, :] * p2 + w[1:2, :] * p1 + w[2:3, :] * u + bconv_ref[:, cols]

    y = jnp.zeros((ts, D_MODEL), F32)
    for c0 in range(0, D_FF, FF_CHUNK):
        u_val = conv(_dot(h2, wup_ref[:, c0:c0 + FF_CHUNK]), c0)
        u_gate = conv(_dot(h2, wup_ref[:, D_FF + c0:D_FF + c0 + FF_CHUNK]), D_FF + c0)
        act = (_gelu_tanh(u_gate) * u_val).astype(BF16)
        y = y + _dot(act, wdown_ref[c0:c0 + FF_CHUNK, :])
    x2 = x1 + gt_f * y
    o_ref[0] = _rms(x2) * gfin_ref[...]


def _post(x, oa, ob, proj, mod, g_ffn, g_final, wbm, wbs, wout, wup, wconv, bconv, wdown):
    b, s, d = x.shape
    ts = POST_TS
    ga_blk = 6 * MIX_WIDTH // D_MODEL
    tok = lambda i, j: (i, j, 0)
    const2 = lambda i, j: (0, 0)
    resident = functools.partial(pl.BlockSpec, index_map=const2, pipeline_mode=pl.Buffered(1))
    return pl.pallas_call(
        _post_kernel,
        grid=(b, s // ts),
        in_specs=[pl.BlockSpec((1, ts, d), tok),
                  pl.BlockSpec((1, ts, MIX_WIDTH), tok),
                  pl.BlockSpec((1, ts, MIX_WIDTH), tok),
                  pl.BlockSpec((1, ts, d), lambda i, j: (i, j, ga_blk)),
                  pl.BlockSpec((1, ts, d), lambda i, j: (i, j, ga_blk + 1)),
                  pl.BlockSpec((1, 6, d), lambda i, j: (i, 0, 0)),
                  pl.BlockSpec((1, d), const2),
                  pl.BlockSpec((1, d), const2),
                  resident((MIX_WIDTH, d)),
                  resident((MIX_WIDTH, d)),
                  resident((d, d)),
                  resident((d, 2 * D_FF)),
                  pl.BlockSpec((3, 2 * D_FF), const2),
                  pl.BlockSpec((1, 2 * D_FF), const2),
                  resident((D_FF, d))],
        out_specs=pl.BlockSpec((1, ts, d), tok),
        out_shape=jax.ShapeDtypeStruct((b, s, d), x.dtype),
        scratch_shapes=[pltpu.VMEM((HALO, 2 * D_FF), F32),
                        pltpu.VMEM((HALO + ts, FF_CHUNK), F32)],
        compiler_params=pltpu.CompilerParams(dimension_semantics=("arbitrary", "arbitrary"),
                                             vmem_limit_bytes=VMEM_LIMIT),
        name="post",
    )(x, oa, ob, proj, proj, mod, g_ffn.reshape(1, d), g_final.reshape(1, d),
      wbm, wbs, wout, wup, wconv, bconv.reshape(1, 2 * D_FF), wdown)


@jax.jit
def kernel(x, c, w_ada, b_ada, g_mix, w_in, w_br_moba, w_br_sb, w_out, rel_bias, g_ffn, w_up,
           w_conv, b_conv, w_down, g_final):
    assert w_ada.shape[0] == 1, "the final rms_norm is fused into the single layer's last kernel"
    l = 0
    bias = _bias_tiles(rel_bias)
    mod = _ada(c, w_ada[l], b_ada[l]).reshape(x.shape[0], 6, D_MODEL)
    proj = _proj(x, mod, g_mix[l], w_in[l].astype(BF16))
    oa = _moba(proj, bias)
    ob = _sb(proj)
    return _post(x, oa, ob, proj, mod, g_ffn[l], g_final,
                 w_br_moba[l].astype(BF16), w_br_sb[l].astype(BF16), w_out[l].astype(BF16),
                 w_up[l].astype(BF16), w_conv[l], b_conv[l], w_down[l].astype(BF16))
```

```python
import functools
import math

import numpy as np
import jax
import jax.numpy as jnp
from jax import lax
from jax.experimental import pallas as pl
from jax.experimental.pallas import tpu as pltpu

D_MODEL = 1024
HEAD_DIM = 64
N_HEADS = 8
MIX_WIDTH = N_HEADS * HEAD_DIM
IN_WIDTH = 6 * MIX_WIDTH + 2 * D_MODEL
MOBA_BLOCK = 256
MOBA_TOPK = 3
D_FF = 2816
REL_BUCKETS = 32
REL_MAX_DIST = 128
NORM_EPS = 1e-6
NEG_INF = -1e30
LOG2E = math.log2(math.e)

LANES = 128
PAIR = LANES // HEAD_DIM
N_PAIRS = N_HEADS // PAIR
BLK = MOBA_BLOCK
VMEM_LIMIT = 56 * 1024 * 1024

F32 = jnp.float32
BF16 = jnp.bfloat16


def _dot(a, b):
    return jnp.dot(a, b, preferred_element_type=F32)


def _dot_nt(a, b):
    return lax.dot_general(a, b, (((1,), (1,)), ((), ())), preferred_element_type=F32)


def _rms(x):
    return x * lax.rsqrt(jnp.mean(x * x, axis=-1, keepdims=True) + NORM_EPS)


def _ada_kernel(c_ref, w_ref, b_ref, o_ref):
    c = c_ref[...]
    ca = c * jax.nn.sigmoid(c)
    o_ref[...] = jnp.dot(ca, w_ref[...], precision=lax.Precision.HIGHEST,
                         preferred_element_type=F32) + b_ref[...]


def _ada(c, w_ada, b_ada):
    b, d = c.shape
    n = w_ada.shape[1]
    tn = D_MODEL
    return pl.pallas_call(
        _ada_kernel,
        grid=(n // tn,),
        in_specs=[pl.BlockSpec((b, d), lambda j: (0, 0)),
                  pl.BlockSpec((d, tn), lambda j: (0, j)),
                  pl.BlockSpec((1, tn), lambda j: (0, j))],
        out_specs=pl.BlockSpec((b, tn), lambda j: (0, j)),
        out_shape=jax.ShapeDtypeStruct((b, n), F32),
        compiler_params=pltpu.CompilerParams(dimension_semantics=("arbitrary",),
                                             vmem_limit_bytes=VMEM_LIMIT),
        name="ada",
    )(c, w_ada, b_ada.reshape(1, n))


def _bucket_tiles():
    def bucket(rel):
        n = np.maximum(rel, 0)
        max_exact = REL_BUCKETS // 2
        n_f = np.maximum(n, max_exact).astype(np.float32)
        large = max_exact + (np.log(n_f / np.float32(max_exact)) / np.float32(math.log(REL_MAX_DIST / max_exact))
                             * np.float32(REL_BUCKETS - max_exact)).astype(np.int32)
        large = np.minimum(large, REL_BUCKETS - 1)
        return np.where(n < max_exact, n, large).astype(np.int32)
    qi = np.arange(BLK)[:, None]
    kj = np.arange(BLK)[None, :]
    t0 = np.where(qi >= kj, bucket(qi - kj), REL_BUCKETS)
    t1 = bucket(qi - kj + BLK)
    far = bucket(np.array([2 * BLK - (BLK - 1)]))
    assert int(far[0]) == REL_BUCKETS - 1
    return np.stack([t0, t1]).astype(np.int32)


def _bias_kernel(tab_ref, bkt_ref, o_ref):
    h = pl.program_id(0)
    for t in range(2):
        b = bkt_ref[t]
        acc = jnp.full((BLK, BLK), NEG_INF, F32)
        for k in range(REL_BUCKETS):
            acc = jnp.where(b == k, tab_ref[h, k], acc)
        o_ref[0, t] = acc
    o_ref[0, 2] = jnp.full((BLK, BLK), tab_ref[h, REL_BUCKETS - 1], F32)


def _bias_tiles(rel_bias):
    bkt = jnp.asarray(_bucket_tiles())
    return pl.pallas_call(
        _bias_kernel,
        grid=(N_HEADS,),
        in_specs=[pl.BlockSpec(memory_space=pltpu.SMEM),
                  pl.BlockSpec((2, BLK, BLK), lambda h: (0, 0, 0))],
        out_specs=pl.BlockSpec((1, 3, BLK, BLK), lambda h: (h, 0, 0, 0)),
        out_shape=jax.ShapeDtypeStruct((N_HEADS, 3, BLK, BLK), F32),
        compiler_params=pltpu.CompilerParams(dimension_semantics=("arbitrary",)),
        name="bias_tiles",
    )(rel_bias, bkt)


PROJ_TS = 256
PROJ_NC = 512


def _proj_kernel(x_ref, mod_ref, g_ref, w_ref, o_ref):
    x = x_ref[0]
    sh = mod_ref[0, 0:1, :]
    sc = mod_ref[0, 1:2, :]
    h = (_rms(x) * g_ref[...] * (1.0 + sc) + sh).astype(BF16)
    scale = HEAD_DIM ** -0.5
    for n0 in range(0, IN_WIDTH, PROJ_NC):
        r = _dot(h, w_ref[:, n0:n0 + PROJ_NC])
        if n0 == 0:
            r = r * scale
        elif n0 == 3 * MIX_WIDTH:
            r = r * (scale * LOG2E)
        elif n0 >= 6 * MIX_WIDTH:
            r = jax.nn.sigmoid(r)
        o_ref[0, :, n0:n0 + PROJ_NC] = r.astype(BF16)


def _proj(x, mod, g_mix, w_in_bf16):
    b, s, d = x.shape
    ts = PROJ_TS
    return pl.pallas_call(
        _proj_kernel,
        grid=(b, s // ts),
        in_specs=[pl.BlockSpec((1, ts, d), lambda i, j: (i, j, 0)),
                  pl.BlockSpec((1, 6, d), lambda i, j: (i, 0, 0)),
                  pl.BlockSpec((1, d), lambda i, j: (0, 0)),
                  pl.BlockSpec((d, IN_WIDTH), lambda i, j: (0, 0), pipeline_mode=pl.Buffered(1))],
        out_specs=pl.BlockSpec((1, ts, IN_WIDTH), lambda i, j: (i, j, 0)),
        out_shape=jax.ShapeDtypeStruct((b, s, IN_WIDTH), BF16),
        compiler_params=pltpu.CompilerParams(dimension_semantics=("arbitrary", "arbitrary"),
                                             vmem_limit_bytes=VMEM_LIMIT),
        name="proj",
    )(x, mod, g_mix.reshape(1, d), w_in_bf16)


def _head_lanes(shape, h):
    lane = lax.broadcasted_iota(jnp.int32, shape, len(shape) - 1)
    return (lane >= h * HEAD_DIM) & (lane < (h + 1) * HEAD_DIM)


def _rows(blk):
    return slice(blk * BLK, (blk + 1) * BLK)


def _run_chains(chains, step):
    states = [None] * len(chains)
    for k in range(max(len(c) for c in chains)):
        for ci, chain in enumerate(chains):
            if k < len(chain):
                states[ci] = step(chain[k], states[ci])
    return states


def _pair_chains(t, n_blk):
    qis = (t, n_blk - 1 - t)
    return qis, [[(qi, j, h) for j in range(qi, -1, -1)] for qi in qis for h in range(PAIR)]


def _store_pair(o_ref, qi, out0, out1):
    o_ref[0, _rows(qi), :] = jnp.where(_head_lanes((BLK, LANES), 0), out0, out1).astype(o_ref.dtype)


def _moba_prepare(q_ref, k_ref, qaug_ref, kaug_ref, pen_ref):
    seq = k_ref.shape[1]
    n_blk = seq // BLK
    k = k_ref[0]
    q = q_ref[0]
    sub = 16
    r = lax.broadcasted_iota(jnp.int32, (sub, seq), 0)
    c = lax.broadcasted_iota(jnp.int32, (sub, seq), 1)
    avg = jnp.where(c // BLK == r, 1.0 / BLK, 0.0).astype(BF16)
    kmean = _dot(avg, k)
    row = lax.broadcasted_iota(jnp.int32, (seq, LANES), 0)
    lane = lax.broadcasted_iota(jnp.int32, (seq, LANES), 1)
    rr = lax.broadcasted_iota(jnp.int32, (BLK, BLK), 0)
    cc = lax.broadcasted_iota(jnp.int32, (BLK, BLK), 1)
    eye = jnp.where(rr == cc, 1.0, 0.0).astype(BF16)
    jidx = lax.broadcasted_iota(jnp.int32, (8, seq), 0)
    qblk = lax.broadcasted_iota(jnp.int32, (8, seq), 1) // BLK
    for h in range(PAIR):
        ind0 = (1 - h) * HEAD_DIM
        ind = (lane - ind0) == (row // BLK)
        kaug_ref[h] = jnp.where(_head_lanes((seq, LANES), h), k, jnp.where(ind, 1.0, 0.0).astype(BF16))
        kmh = jnp.where(_head_lanes((sub, LANES), h), kmean, 0.0)
        hi = kmh.astype(BF16)
        lo = (kmh - hi.astype(F32)).astype(BF16)
        g8 = (_dot_nt(hi, q) + _dot_nt(lo, q))[0:8, :]
        cnt = jnp.zeros((8, seq), jnp.int32)
        for jp in range(n_blk):
            rowv = g8[jp:jp + 1, :]
            beats = (rowv > g8) | ((rowv == g8) & (jp < jidx))
            cnt = cnt + jnp.where(beats & (jp < qblk), 1, 0)
        sel = (jidx == qblk) | ((jidx < qblk) & (cnt < MOBA_TOPK))
        pen_ref[...] = jnp.zeros(pen_ref.shape, F32)
        pen_ref[ind0:ind0 + 8, :] = jnp.where(sel, 0.0, NEG_INF)
        for b in range(n_blk):
            pen_cols = _dot_nt(eye, pen_ref[:, _rows(b)].astype(BF16))
            qaug_ref[h, _rows(b), :] = jnp.where(_head_lanes((BLK, LANES), h), q[_rows(b), :],
                                                 pen_cols.astype(BF16))


def _moba_pair(t, n_blk, v_ref, bias_ref, o_ref, qaug_ref, kaug_ref):
    def step(args, st):
        qi, j, h = args
        s = _dot_nt(qaug_ref[h, _rows(qi), :], kaug_ref[h, _rows(j), :]) + bias_ref[h, min(qi - j, 2)]
        m_blk = jnp.max(s, axis=-1, keepdims=True)
        vj = v_ref[0, _rows(j), :]
        if st is None:
            p = jnp.exp(s - m_blk)
            return m_blk, jnp.sum(p, axis=-1, keepdims=True), _dot(p.astype(BF16), vj)
        m, l, acc = st
        m_new = jnp.maximum(m, m_blk)
        alpha = jnp.exp(m - m_new)
        p = jnp.exp(s - m_new)
        return (m_new, alpha * l + jnp.sum(p, axis=-1, keepdims=True),
                alpha * acc + _dot(p.astype(BF16), vj))

    qis, chains = _pair_chains(t, n_blk)
    states = _run_chains(chains, step)
    for n, qi in enumerate(qis):
        (_, l0, a0), (_, l1, a1) = states[PAIR * n], states[PAIR * n + 1]
        _store_pair(o_ref, qi, a0 / l0, a1 / l1)


def _moba_kernel(q_ref, k_ref, v_ref, bias_ref, o_ref, qaug_ref, kaug_ref, pen_ref):
    t = pl.program_id(2)
    n_blk = k_ref.shape[1] // BLK

    @pl.when(t == 0)
    def _prepare():
        _moba_prepare(q_ref, k_ref, qaug_ref, kaug_ref, pen_ref)

    for tt in range(n_blk // 2):
        pl.when(t == tt)(functools.partial(_moba_pair, tt, n_blk, v_ref, bias_ref, o_ref, qaug_ref, kaug_ref))


def _attn_specs(s, q_off, k_off, v_off):
    full = lambda off: pl.BlockSpec((1, s, LANES), lambda bi, p, t: (bi, 0, off + p))
    return [full(q_off), full(k_off), full(v_off)], pl.BlockSpec((1, s, LANES), lambda bi, p, t: (bi, 0, p))


def _moba(proj, bias):
    b, s, _ = proj.shape
    n_blk = s // BLK
    assert s % BLK == 0 and n_blk % 2 == 0 and n_blk <= 8
    in_specs, out_spec = _attn_specs(s, 0, MIX_WIDTH // LANES, 2 * MIX_WIDTH // LANES)
    return pl.pallas_call(
        _moba_kernel,
        grid=(b, N_PAIRS, n_blk // 2),
        in_specs=in_specs + [pl.BlockSpec((PAIR, 3, BLK, BLK), lambda bi, p, t: (p, 0, 0, 0))],
        out_specs=out_spec,
        out_shape=jax.ShapeDtypeStruct((b, s, MIX_WIDTH), BF16),
        scratch_shapes=[pltpu.VMEM((PAIR, s, LANES), BF16),
                        pltpu.VMEM((PAIR, s, LANES), BF16),
                        pltpu.VMEM((LANES, s), F32)],
        compiler_params=pltpu.CompilerParams(dimension_semantics=("arbitrary",) * 3,
                                             vmem_limit_bytes=VMEM_LIMIT),
        name="moba",
    )(proj, proj, proj, bias)


SB_MASK = -1e9


def _sb_mask_tiles():
    r = np.arange(BLK)[:, None]
    c = np.arange(BLK)[None, :]
    tri = np.where(c < r, 0.0, SB_MASK)
    return np.stack([np.zeros((BLK, PAIR * BLK)), np.tile(tri, (1, PAIR))]).astype(np.float32)


def _sb_kernel(q_ref, k_ref, v_ref, mask_ref, o_ref, ks_ref, vs_ref, hl_ref, lb_ref, a_ref, acc_ref):
    seq = k_ref.shape[1]
    n_blk = seq // BLK
    n_tiles = n_blk * (n_blk + 1) // 2
    wide = PAIR * BLK
    rr = lax.broadcasted_iota(jnp.int32, (2 * BLK, BLK), 0) % BLK
    cc = lax.broadcasted_iota(jnp.int32, (2 * BLK, BLK), 1)
    upper2 = jnp.where(rr > cc, 1.0, 0.0).astype(BF16)
    sign_bit = jnp.uint32(0x80000000)

    for h in range(PAIR):
        in_head = _head_lanes((BLK, LANES), h)
        for blk in range(n_blk):
            ks_ref[blk, h * BLK:(h + 1) * BLK, :] = jnp.where(in_head, k_ref[0, _rows(blk), :], jnp.zeros((), BF16))
            vs_ref[blk, h * BLK:(h + 1) * BLK, :] = jnp.where(in_head, v_ref[0, _rows(blk), :], jnp.zeros((), BF16))

    def rows(blk):
        return pl.ds(pl.multiple_of(blk * BLK, BLK), BLK)

    def cols(h):
        return slice(h * BLK, (h + 1) * BLK)

    def next_tile(t):
        qi, j = t
        last = j == 0
        return jnp.where(last, qi + 1, qi), jnp.where(last, qi + 1, j - 1)

    def stage1(t, slot, run):
        qi, j = t
        diag = qi == j
        z = _dot_nt(q_ref[0, rows(qi), :], ks_ref[j]) + mask_ref[jnp.where(diag, 1, 0)]
        neg_abs = lax.bitcast_convert_type(lax.bitcast_convert_type(z, jnp.uint32) | sign_bit, F32)
        sp = jnp.log(1.0 + jnp.exp2(neg_abs)) * LOG2E
        log_beta = jnp.minimum(z, 0.0) - sp
        log_keep = log_beta - z
        hi = log_keep.astype(BF16)
        lo = (log_keep - hi.astype(F32)).astype(BF16)
        lb_ref[slot] = log_beta
        keep = jnp.where(diag, 0.0, 1.0)
        through, before = [], []
        for h in range(PAIR):
            hl_ref[slot, h, :, 0:BLK] = hi[:, cols(h)]
            hl_ref[slot, h, :, BLK:2 * BLK] = lo[:, cols(h)]
            prev = run[h] * keep
            before.append(prev)
            through.append(prev + jnp.sum(log_keep[:, cols(h)], axis=-1, keepdims=True))
        return through, before

    def stage2(slot, before):
        for h in range(PAIR):
            later = _dot(hl_ref[slot, h], upper2)
            a_ref[slot, :, cols(h)] = jnp.exp2(lb_ref[slot, :, cols(h)] + before[h] + later).astype(BF16)

    def stage3(t, slot):
        qi, j = t
        pv = _dot(a_ref[slot], vs_ref[j])
        acc = jnp.where(qi == j, pv, acc_ref[...] + pv)
        acc_ref[...] = acc
        o_ref[0, rows(qi), :] = acc.astype(o_ref.dtype)

    zero = jnp.zeros((BLK, 1), F32)
    t0 = (jnp.int32(0), jnp.int32(0))
    t1 = next_tile(t0)
    run, before0 = stage1(t0, 0, [zero] * PAIR)
    run, before1 = stage1(t1, 1, run)
    stage2(0, before0)

    def body(_, carry):
        t_cur, t_m1, t_m2, run, before_m1 = carry
        run, before_cur = stage1(t_cur, 0, run)
        stage2(1, before_m1)
        stage3(t_m2, 0)
        t_nxt = next_tile(t_cur)
        run, before_nxt = stage1(t_nxt, 1, run)
        stage2(0, before_cur)
        stage3(t_m1, 1)
        return next_tile(t_nxt), t_nxt, t_cur, run, before_nxt

    assert n_tiles % 2 == 0
    _, t_m1, t_m2, _, before_m1 = lax.fori_loop(0, (n_tiles - 2) // 2, body,
                                                (next_tile(t1), t1, t0, run, before1))
    stage2(1, before_m1)
    stage3(t_m2, 0)
    stage3(t_m1, 1)


def _sb(proj):
    b, s, _ = proj.shape
    assert s % BLK == 0
    base = 3 * MIX_WIDTH // LANES
    full = lambda off: pl.BlockSpec((1, s, LANES), lambda bi, p: (bi, 0, off + p))
    return pl.pallas_call(
        _sb_kernel,
        grid=(b, N_PAIRS),
        in_specs=[full(base), full(base + MIX_WIDTH // LANES), full(base + 2 * MIX_WIDTH // LANES),
                  pl.BlockSpec((2, BLK, PAIR * BLK), lambda bi, p: (0, 0, 0))],
        out_specs=pl.BlockSpec((1, s, LANES), lambda bi, p: (bi, 0, p)),
        out_shape=jax.ShapeDtypeStruct((b, s, MIX_WIDTH), BF16),
        scratch_shapes=[pltpu.VMEM((s // BLK, PAIR * BLK, LANES), BF16),
                        pltpu.VMEM((s // BLK, PAIR * BLK, LANES), BF16),
                        pltpu.VMEM((2, PAIR, BLK, 2 * BLK), BF16),
                        pltpu.VMEM((2, BLK, PAIR * BLK), F32),
                        pltpu.VMEM((2, BLK, PAIR * BLK), BF16),
                        pltpu.VMEM((BLK, LANES), F32)],
        compiler_params=pltpu.CompilerParams(dimension_semantics=("arbitrary",) * 2,
                                             vmem_limit_bytes=VMEM_LIMIT),
        name="sb",
    )(proj, proj, proj, jnp.asarray(_sb_mask_tiles()))


POST_TS = 256
FF_CHUNK = D_FF // 2
HALO = 8


def _gelu_tanh(x):
    return 0.5 * x * (1.0 + jnp.tanh(math.sqrt(2.0 / math.pi) * (x + 0.044715 * (x * x * x))))


def _post_kernel(x_ref, oa_ref, ob_ref, ga_ref, gb_ref, mod_ref, gffn_ref, gfin_ref,
                 wbm_ref, wbs_ref, wout_ref, wup_ref, wconv_ref, bconv_ref, wdown_ref,
                 o_ref, halo_ref, ubuf_ref):
    ts = x_ref.shape[1]

    @pl.when(pl.program_id(1) == 0)
    def _zero_halo():
        halo_ref[...] = jnp.zeros_like(halo_ref)

    gt_m = mod_ref[0, 2:3, :]
    sh_f = mod_ref[0, 3:4, :]
    sc_f = mod_ref[0, 4:5, :]
    gt_f = mod_ref[0, 5:6, :]

    ya = _dot(oa_ref[0], wbm_ref[...])
    yb = _dot(ob_ref[0], wbs_ref[...])
    mix = (ga_ref[0].astype(F32) * ya + gb_ref[0].astype(F32) * yb).astype(BF16)
    x1 = x_ref[0] + gt_m * _dot(mix, wout_ref[...])
    h2 = (_rms(x1) * gffn_ref[...] * (1.0 + sc_f) + sh_f).astype(BF16)

    def conv(u, c0):
        cols = slice(c0, c0 + FF_CHUNK)
        ubuf_ref[0:HALO, :] = halo_ref[:, cols]
        ubuf_ref[HALO:HALO + ts, :] = u
        halo_ref[:, cols] = u[ts - HALO:ts, :]
        p1 = ubuf_ref[HALO - 1:HALO - 1 + ts, :]
        p2 = ubuf_ref[HALO - 2:HALO - 2 + ts, :]
        w = wconv_ref[:, cols]
        return w[0:1, :] * p2 + w[1:2, :] * p1 + w[2:3, :] * u + bconv_ref[:, cols]

    y = jnp.zeros((ts, D_MODEL), F32)
    for c0 in range(0, D_FF, FF_CHUNK):
        u_val = conv(_dot(h2, wup_ref[:, c0:c0 + FF_CHUNK]), c0)
        u_gate = conv(_dot(h2, wup_ref[:, D_FF + c0:D_FF + c0 + FF_CHUNK]), D_FF + c0)
        act = (_gelu_tanh(u_gate) * u_val).astype(BF16)
        y = y + _dot(act, wdown_ref[c0:c0 + FF_CHUNK, :])
    x2 = x1 + gt_f * y
    o_ref[0] = _rms(x2) * gfin_ref[...]


def _post(x, oa, ob, proj, mod, g_ffn, g_final, wbm, wbs, wout, wup, wconv, bconv, wdown):
    b, s, d = x.shape
    ts = POST_TS
    ga_blk = 6 * MIX_WIDTH // D_MODEL
    tok = lambda i, j: (i, j, 0)
    const2 = lambda i, j: (0, 0)
    resident = functools.partial(pl.BlockSpec, index_map=const2, pipeline_mode=pl.Buffered(1))
    return pl.pallas_call(
        _post_kernel,
        grid=(b, s // ts),
        in_specs=[pl.BlockSpec((1, ts, d), tok),
                  pl.BlockSpec((1, ts, MIX_WIDTH), tok),
                  pl.BlockSpec((1, ts, MIX_WIDTH), tok),
                  pl.BlockSpec((1, ts, d), lambda i, j: (i, j, ga_blk)),
                  pl.BlockSpec((1, ts, d), lambda i, j: (i, j, ga_blk + 1)),
                  pl.BlockSpec((1, 6, d), lambda i, j: (i, 0, 0)),
                  pl.BlockSpec((1, d), const2),
                  pl.BlockSpec((1, d), const2),
                  resident((MIX_WIDTH, d)),
                  resident((MIX_WIDTH, d)),
                  resident((d, d)),
                  resident((d, 2 * D_FF)),
                  pl.BlockSpec((3, 2 * D_FF), const2),
                  pl.BlockSpec((1, 2 * D_FF), const2),
                  resident((D_FF, d))],
        out_specs=pl.BlockSpec((1, ts, d), tok),
        out_shape=jax.ShapeDtypeStruct((b, s, d), x.dtype),
        scratch_shapes=[pltpu.VMEM((HALO, 2 * D_FF), F32),
                        pltpu.VMEM((HALO + ts, FF_CHUNK), F32)],
        compiler_params=pltpu.CompilerParams(dimension_semantics=("arbitrary", "arbitrary"),
                                             vmem_limit_bytes=VMEM_LIMIT),
        name="post",
    )(x, oa, ob, proj, proj, mod, g_ffn.reshape(1, d), g_final.reshape(1, d),
      wbm, wbs, wout, wup, wconv, bconv.reshape(1, 2 * D_FF), wdown)


@jax.jit
def kernel(x, c, w_ada, b_ada, g_mix, w_in, w_br_moba, w_br_sb, w_out, rel_bias, g_ffn, w_up,
           w_conv, b_conv, w_down, g_final):
    assert w_ada.shape[0] == 1, "the final rms_norm is fused into the single layer's last kernel"
    l = 0
    bias = _bias_tiles(rel_bias)
    mod = _ada(c, w_ada[l], b_ada[l]).reshape(x.shape[0], 6, D_MODEL)
    proj = _proj(x, mod, g_mix[l], w_in[l].astype(BF16))
    oa = _moba(proj, bias)
    ob = _sb(proj)
    return _post(x, oa, ob, proj, mod, g_ffn[l], g_final,
                 w_br_moba[l].astype(BF16), w_br_sb[l].astype(BF16), w_out[l].astype(BF16),
                 w_up[l].astype(BF16), w_conv[l], b_conv[l], w_down[l].astype(BF16))
```

```python
import functools
import math

import numpy as np
import jax
import jax.numpy as jnp
from jax import lax
from jax.experimental import pallas as pl
from jax.experimental.pallas import tpu as pltpu

D_MODEL = 1024
HEAD_DIM = 64
N_HEADS = 8
MIX_WIDTH = N_HEADS * HEAD_DIM
IN_WIDTH = 6 * MIX_WIDTH + 2 * D_MODEL
MOBA_BLOCK = 256
MOBA_TOPK = 3
D_FF = 2816
REL_BUCKETS = 32
REL_MAX_DIST = 128
NORM_EPS = 1e-6
NEG_INF = -1e30
LOG2E = math.log2(math.e)

LANES = 128
PAIR = LANES // HEAD_DIM
N_PAIRS = N_HEADS // PAIR
BLK = MOBA_BLOCK
VMEM_LIMIT = 56 * 1024 * 1024

F32 = jnp.float32
BF16 = jnp.bfloat16


def _dot(a, b):
    return jnp.dot(a, b, preferred_element_type=F32)


def _dot_nt(a, b):
    return lax.dot_general(a, b, (((1,), (1,)), ((), ())), preferred_element_type=F32)


def _rms(x):
    return x * lax.rsqrt(jnp.mean(x * x, axis=-1, keepdims=True) + NORM_EPS)


def _ada_kernel(c_ref, w_ref, b_ref, o_ref):
    c = c_ref[...]
    ca = c * jax.nn.sigmoid(c)
    o_ref[...] = jnp.dot(ca, w_ref[...], precision=lax.Precision.HIGHEST,
                         preferred_element_type=F32) + b_ref[...]


def _ada(c, w_ada, b_ada):
    b, d = c.shape
    n = w_ada.shape[1]
    tn = D_MODEL
    return pl.pallas_call(
        _ada_kernel,
        grid=(n // tn,),
        in_specs=[pl.BlockSpec((b, d), lambda j: (0, 0)),
                  pl.BlockSpec((d, tn), lambda j: (0, j)),
                  pl.BlockSpec((1, tn), lambda j: (0, j))],
        out_specs=pl.BlockSpec((b, tn), lambda j: (0, j)),
        out_shape=jax.ShapeDtypeStruct((b, n), F32),
        compiler_params=pltpu.CompilerParams(dimension_semantics=("arbitrary",),
                                             vmem_limit_bytes=VMEM_LIMIT),
        name="ada",
    )(c, w_ada, b_ada.reshape(1, n))


def _bucket_tiles():
    def bucket(rel):
        n = np.maximum(rel, 0)
        max_exact = REL_BUCKETS // 2
        n_f = np.maximum(n, max_exact).astype(np.float32)
        large = max_exact + (np.log(n_f / np.float32(max_exact)) / np.float32(math.log(REL_MAX_DIST / max_exact))
                             * np.float32(REL_BUCKETS - max_exact)).astype(np.int32)
        large = np.minimum(large, REL_BUCKETS - 1)
        return np.where(n < max_exact, n, large).astype(np.int32)
    qi = np.arange(BLK)[:, None]
    kj = np.arange(BLK)[None, :]
    t0 = np.where(qi >= kj, bucket(qi - kj), REL_BUCKETS)
    t1 = bucket(qi - kj + BLK)
    far = bucket(np.array([2 * BLK - (BLK - 1)]))
    assert int(far[0]) == REL_BUCKETS - 1
    return np.stack([t0, t1]).astype(np.int32)


def _bias_kernel(tab_ref, bkt_ref, o_ref):
    h = pl.program_id(0)
    for t in range(2):
        b = bkt_ref[t]
        acc = jnp.full((BLK, BLK), NEG_INF, F32)
        for k in range(REL_BUCKETS):
            acc = jnp.where(b == k, tab_ref[h, k], acc)
        o_ref[0, t] = acc
    o_ref[0, 2] = jnp.full((BLK, BLK), tab_ref[h, REL_BUCKETS - 1], F32)


def _bias_tiles(rel_bias):
    bkt = jnp.asarray(_bucket_tiles())
    return pl.pallas_call(
        _bias_kernel,
        grid=(N_HEADS,),
        in_specs=[pl.BlockSpec(memory_space=pltpu.SMEM),
                  pl.BlockSpec((2, BLK, BLK), lambda h: (0, 0, 0))],
        out_specs=pl.BlockSpec((1, 3, BLK, BLK), lambda h: (h, 0, 0, 0)),
        out_shape=jax.ShapeDtypeStruct((N_HEADS, 3, BLK, BLK), F32),
        compiler_params=pltpu.CompilerParams(dimension_semantics=("arbitrary",)),
        name="bias_tiles",
    )(rel_bias, bkt)


PROJ_TS = 512
PROJ_NC = 512


def _proj_kernel(x_ref, mod_ref, g_ref, w_ref, o_ref):
    x = x_ref[0]
    sh = mod_ref[0, 0:1, :]
    sc = mod_ref[0, 1:2, :]
    h = (_rms(x) * g_ref[...] * (1.0 + sc) + sh).astype(BF16)
    scale = HEAD_DIM ** -0.5
    for n0 in range(0, IN_WIDTH, PROJ_NC):
        r = _dot(h, w_ref[:, n0:n0 + PROJ_NC])
        if n0 == 0:
            r = r * scale
        elif n0 == 3 * MIX_WIDTH:
            r = r * (scale * LOG2E)
        elif n0 >= 6 * MIX_WIDTH:
            r = jax.nn.sigmoid(r)
        o_ref[0, :, n0:n0 + PROJ_NC] = r.astype(BF16)


def _proj(x, mod, g_mix, w_in_bf16):
    b, s, d = x.shape
    ts = PROJ_TS
    return pl.pallas_call(
        _proj_kernel,
        grid=(b, s // ts),
        in_specs=[pl.BlockSpec((1, ts, d), lambda i, j: (i, j, 0)),
                  pl.BlockSpec((1, 6, d), lambda i, j: (i, 0, 0)),
                  pl.BlockSpec((1, d), lambda i, j: (0, 0)),
                  pl.BlockSpec((d, IN_WIDTH), lambda i, j: (0, 0), pipeline_mode=pl.Buffered(1))],
        out_specs=pl.BlockSpec((1, ts, IN_WIDTH), lambda i, j: (i, j, 0)),
        out_shape=jax.ShapeDtypeStruct((b, s, IN_WIDTH), BF16),
        compiler_params=pltpu.CompilerParams(dimension_semantics=("arbitrary", "arbitrary"),
                                             vmem_limit_bytes=VMEM_LIMIT),
        name="proj",
    )(x, mod, g_mix.reshape(1, d), w_in_bf16)


def _head_lanes(shape, h):
    lane = lax.broadcasted_iota(jnp.int32, shape, len(shape) - 1)
    return (lane >= h * HEAD_DIM) & (lane < (h + 1) * HEAD_DIM)


def _rows(blk):
    return slice(blk * BLK, (blk + 1) * BLK)


def _run_chains(chains, step):
    states = [None] * len(chains)
    for k in range(max(len(c) for c in chains)):
        for ci, chain in enumerate(chains):
            if k < len(chain):
                states[ci] = step(chain[k], states[ci])
    return states


def _pair_chains(t, n_blk):
    qis = (t, n_blk - 1 - t)
    return qis, [[(qi, j, h) for j in range(qi, -1, -1)] for qi in qis for h in range(PAIR)]


def _store_pair(o_ref, qi, out0, out1):
    o_ref[0, _rows(qi), :] = jnp.where(_head_lanes((BLK, LANES), 0), out0, out1).astype(o_ref.dtype)


def _moba_prepare(q_ref, k_ref, qaug_ref, kaug_ref, pen_ref):
    seq = k_ref.shape[1]
    n_blk = seq // BLK
    k = k_ref[0]
    q = q_ref[0]
    sub = 16
    r = lax.broadcasted_iota(jnp.int32, (sub, seq), 0)
    c = lax.broadcasted_iota(jnp.int32, (sub, seq), 1)
    avg = jnp.where(c // BLK == r, 1.0 / BLK, 0.0).astype(BF16)
    kmean = _dot(avg, k)
    row = lax.broadcasted_iota(jnp.int32, (seq, LANES), 0)
    lane = lax.broadcasted_iota(jnp.int32, (seq, LANES), 1)
    rr = lax.broadcasted_iota(jnp.int32, (BLK, BLK), 0)
    cc = lax.broadcasted_iota(jnp.int32, (BLK, BLK), 1)
    eye = jnp.where(rr == cc, 1.0, 0.0).astype(BF16)
    jidx = lax.broadcasted_iota(jnp.int32, (8, seq), 0)
    qblk = lax.broadcasted_iota(jnp.int32, (8, seq), 1) // BLK
    for h in range(PAIR):
        ind0 = (1 - h) * HEAD_DIM
        ind = (lane - ind0) == (row // BLK)
        kaug_ref[h] = jnp.where(_head_lanes((seq, LANES), h), k, jnp.where(ind, 1.0, 0.0).astype(BF16))
        kmh = jnp.where(_head_lanes((sub, LANES), h), kmean, 0.0)
        hi = kmh.astype(BF16)
        lo = (kmh - hi.astype(F32)).astype(BF16)
        g8 = (_dot_nt(hi, q) + _dot_nt(lo, q))[0:8, :]
        cnt = jnp.zeros((8, seq), jnp.int32)
        for jp in range(n_blk):
            rowv = g8[jp:jp + 1, :]
            beats = (rowv > g8) | ((rowv == g8) & (jp < jidx))
            cnt = cnt + jnp.where(beats & (jp < qblk), 1, 0)
        sel = (jidx == qblk) | ((jidx < qblk) & (cnt < MOBA_TOPK))
        pen_ref[...] = jnp.zeros(pen_ref.shape, F32)
        pen_ref[ind0:ind0 + 8, :] = jnp.where(sel, 0.0, NEG_INF)
        for b in range(n_blk):
            pen_cols = _dot_nt(eye, pen_ref[:, _rows(b)].astype(BF16))
            qaug_ref[h, _rows(b), :] = jnp.where(_head_lanes((BLK, LANES), h), q[_rows(b), :],
                                                 pen_cols.astype(BF16))


def _moba_pair(t, n_blk, v_ref, bias_ref, o_ref, qaug_ref, kaug_ref):
    def step(args, st):
        qi, j, h = args
        s = _dot_nt(qaug_ref[h, _rows(qi), :], kaug_ref[h, _rows(j), :]) + bias_ref[h, min(qi - j, 2)]
        m_blk = jnp.max(s, axis=-1, keepdims=True)
        vj = v_ref[0, _rows(j), :]
        if st is None:
            p = jnp.exp(s - m_blk)
            return m_blk, jnp.sum(p, axis=-1, keepdims=True), _dot(p.astype(BF16), vj)
        m, l, acc = st
        m_new = jnp.maximum(m, m_blk)
        alpha = jnp.exp(m - m_new)
        p = jnp.exp(s - m_new)
        return (m_new, alpha * l + jnp.sum(p, axis=-1, keepdims=True),
                alpha * acc + _dot(p.astype(BF16), vj))

    qis, chains = _pair_chains(t, n_blk)
    states = _run_chains(chains, step)
    for n, qi in enumerate(qis):
        (_, l0, a0), (_, l1, a1) = states[PAIR * n], states[PAIR * n + 1]
        _store_pair(o_ref, qi, a0 / l0, a1 / l1)


def _moba_kernel(q_ref, k_ref, v_ref, bias_ref, o_ref, qaug_ref, kaug_ref, pen_ref):
    t = pl.program_id(2)
    n_blk = k_ref.shape[1] // BLK

    @pl.when(t == 0)
    def _prepare():
        _moba_prepare(q_ref, k_ref, qaug_ref, kaug_ref, pen_ref)

    for tt in range(n_blk // 2):
        pl.when(t == tt)(functools.partial(_moba_pair, tt, n_blk, v_ref, bias_ref, o_ref, qaug_ref, kaug_ref))


def _attn_specs(s, q_off, k_off, v_off):
    full = lambda off: pl.BlockSpec((1, s, LANES), lambda bi, p, t: (bi, 0, off + p))
    return [full(q_off), full(k_off), full(v_off)], pl.BlockSpec((1, s, LANES), lambda bi, p, t: (bi, 0, p))


def _moba(proj, bias):
    b, s, _ = proj.shape
    n_blk = s // BLK
    assert s % BLK == 0 and n_blk % 2 == 0 and n_blk <= 8
    in_specs, out_spec = _attn_specs(s, 0, MIX_WIDTH // LANES, 2 * MIX_WIDTH // LANES)
    return pl.pallas_call(
        _moba_kernel,
        grid=(b, N_PAIRS, n_blk // 2),
        in_specs=in_specs + [pl.BlockSpec((PAIR, 3, BLK, BLK), lambda bi, p, t: (p, 0, 0, 0))],
        out_specs=out_spec,
        out_shape=jax.ShapeDtypeStruct((b, s, MIX_WIDTH), BF16),
        scratch_shapes=[pltpu.VMEM((PAIR, s, LANES), BF16),
                        pltpu.VMEM((PAIR, s, LANES), BF16),
                        pltpu.VMEM((LANES, s), F32)],
        compiler_params=pltpu.CompilerParams(dimension_semantics=("arbitrary",) * 3,
                                             vmem_limit_bytes=VMEM_LIMIT),
        name="moba",
    )(proj, proj, proj, bias)


SB_MASK = -1e9


def _sb_mask_tiles():
    r = np.arange(BLK)[:, None]
    c = np.arange(BLK)[None, :]
    tri = np.where(c < r, 0.0, SB_MASK)
    return np.stack([np.zeros((BLK, PAIR * BLK)), np.tile(tri, (1, PAIR))]).astype(np.float32)


def _sb_kernel(q_ref, k_ref, v_ref, mask_ref, o_ref, ks_ref, vs_ref, hl_ref, lb_ref, a_ref, acc_ref):
    seq = k_ref.shape[1]
    n_blk = seq // BLK
    n_tiles = n_blk * (n_blk + 1) // 2
    wide = PAIR * BLK
    rr = lax.broadcasted_iota(jnp.int32, (2 * BLK, BLK), 0) % BLK
    cc = lax.broadcasted_iota(jnp.int32, (2 * BLK, BLK), 1)
    upper2 = jnp.where(rr > cc, 1.0, 0.0).astype(BF16)
    sign_bit = jnp.uint32(0x80000000)

    for h in range(PAIR):
        in_head = _head_lanes((BLK, LANES), h)
        for blk in range(n_blk):
            ks_ref[blk, h * BLK:(h + 1) * BLK, :] = jnp.where(in_head, k_ref[0, _rows(blk), :], jnp.zeros((), BF16))
            vs_ref[blk, h * BLK:(h + 1) * BLK, :] = jnp.where(in_head, v_ref[0, _rows(blk), :], jnp.zeros((), BF16))

    def rows(blk):
        return pl.ds(pl.multiple_of(blk * BLK, BLK), BLK)

    def cols(h):
        return slice(h * BLK, (h + 1) * BLK)

    def next_tile(t):
        qi, j = t
        last = j == 0
        return jnp.where(last, qi + 1, qi), jnp.where(last, qi + 1, j - 1)

    def stage1(t, slot, run):
        qi, j = t
        diag = qi == j
        z = _dot_nt(q_ref[0, rows(qi), :], ks_ref[j]) + mask_ref[jnp.where(diag, 1, 0)]
        neg_abs = lax.bitcast_convert_type(lax.bitcast_convert_type(z, jnp.uint32) | sign_bit, F32)
        sp = jnp.log(1.0 + jnp.exp2(neg_abs)) * LOG2E
        log_beta = jnp.minimum(z, 0.0) - sp
        log_keep = log_beta - z
        hi = log_keep.astype(BF16)
        lo = (log_keep - hi.astype(F32)).astype(BF16)
        lb_ref[slot] = log_beta
        keep = jnp.where(diag, 0.0, 1.0)
        through, before = [], []
        for h in range(PAIR):
            hl_ref[slot, h, :, 0:BLK] = hi[:, cols(h)]
            hl_ref[slot, h, :, BLK:2 * BLK] = lo[:, cols(h)]
            prev = run[h] * keep
            before.append(prev)
            through.append(prev + jnp.sum(log_keep[:, cols(h)], axis=-1, keepdims=True))
        return through, before

    def stage2(slot, before):
        for h in range(PAIR):
            later = _dot(hl_ref[slot, h], upper2)
            a_ref[slot, :, cols(h)] = jnp.exp2(lb_ref[slot, :, cols(h)] + before[h] + later).astype(BF16)

    def stage3(t, slot):
        qi, j = t
        pv = _dot(a_ref[slot], vs_ref[j])
        acc = jnp.where(qi == j, pv, acc_ref[...] + pv)
        acc_ref[...] = acc
        o_ref[0, rows(qi), :] = acc.astype(o_ref.dtype)

    zero = jnp.zeros((BLK, 1), F32)
    t0 = (jnp.int32(0), jnp.int32(0))
    t1 = next_tile(t0)
    run, before0 = stage1(t0, 0, [zero] * PAIR)
    run, before1 = stage1(t1, 1, run)
    stage2(0, before0)

    def body(_, carry):
        t_cur, t_m1, t_m2, run, before_m1 = carry
        run, before_cur = stage1(t_cur, 0, run)
        stage2(1, before_m1)
        stage3(t_m2, 0)
        t_nxt = next_tile(t_cur)
        run, before_nxt = stage1(t_nxt, 1, run)
        stage2(0, before_cur)
        stage3(t_m1, 1)
        return next_tile(t_nxt), t_nxt, t_cur, run, before_nxt

    assert n_tiles % 2 == 0
    _, t_m1, t_m2, _, before_m1 = lax.fori_loop(0, (n_tiles - 2) // 2, body,
                                                (next_tile(t1), t1, t0, run, before1))
    stage2(1, before_m1)
    stage3(t_m2, 0)
    stage3(t_m1, 1)


def _sb(proj):
    b, s, _ = proj.shape
    assert s % BLK == 0
    base = 3 * MIX_WIDTH // LANES
    full = lambda off: pl.BlockSpec((1, s, LANES), lambda bi, p: (bi, 0, off + p))
    return pl.pallas_call(
        _sb_kernel,
        grid=(b, N_PAIRS),
        in_specs=[full(base), full(base + MIX_WIDTH // LANES), full(base + 2 * MIX_WIDTH // LANES),
                  pl.BlockSpec((2, BLK, PAIR * BLK), lambda bi, p: (0, 0, 0))],
        out_specs=pl.BlockSpec((1, s, LANES), lambda bi, p: (bi, 0, p)),
        out_shape=jax.ShapeDtypeStruct((b, s, MIX_WIDTH), BF16),
        scratch_shapes=[pltpu.VMEM((s // BLK, PAIR * BLK, LANES), BF16),
                        pltpu.VMEM((s // BLK, PAIR * BLK, LANES), BF16),
                        pltpu.VMEM((2, PAIR, BLK, 2 * BLK), BF16),
                        pltpu.VMEM((2, BLK, PAIR * BLK), F32),
                        pltpu.VMEM((2, BLK, PAIR * BLK), BF16),
                        pltpu.VMEM((BLK, LANES), F32)],
        compiler_params=pltpu.CompilerParams(dimension_semantics=("arbitrary",) * 2,
                                             vmem_limit_bytes=VMEM_LIMIT),
        name="sb",
    )(proj, proj, proj, jnp.asarray(_sb_mask_tiles()))


POST_TS = 512
MXU_TILE = 256
FF_SPLIT = (D_FF // MXU_TILE + 1) // 2 * MXU_TILE
FF_CHUNKS = ((0, FF_SPLIT), (FF_SPLIT, D_FF - FF_SPLIT))
HALO = 8


def _gelu_tanh(x):
    return 0.5 * x * (1.0 + jnp.tanh(math.sqrt(2.0 / math.pi) * (x + 0.044715 * (x * x * x))))


def _post_kernel(x_ref, oa_ref, ob_ref, ga_ref, gb_ref, mod_ref, gffn_ref, gfin_ref,
                 wbm_ref, wbs_ref, wout_ref, wup_ref, wconv_ref, bconv_ref, wdown_ref,
                 o_ref, halo_ref, ubuf_ref):
    ts = x_ref.shape[1]

    @pl.when(pl.program_id(1) == 0)
    def _zero_halo():
        halo_ref[...] = jnp.zeros_like(halo_ref)

    gt_m = mod_ref[0, 2:3, :]
    sh_f = mod_ref[0, 3:4, :]
    sc_f = mod_ref[0, 4:5, :]
    gt_f = mod_ref[0, 5:6, :]

    ya = _dot(oa_ref[0], wbm_ref[...])
    yb = _dot(ob_ref[0], wbs_ref[...])
    mix = (ga_ref[0].astype(F32) * ya + gb_ref[0].astype(F32) * yb).astype(BF16)
    x1 = x_ref[0] + gt_m * _dot(mix, wout_ref[...])
    h2 = (_rms(x1) * gffn_ref[...] * (1.0 + sc_f) + sh_f).astype(BF16)

    def conv(c0, width, buf):
        cols = slice(c0, c0 + width)
        u = _dot(h2, wup_ref[:, cols])
        ubuf_ref[buf, 0:HALO, 0:width] = halo_ref[:, cols]
        ubuf_ref[buf, HALO:HALO + ts, 0:width] = u
        halo_ref[:, cols] = u[ts - HALO:ts, :]
        p1 = ubuf_ref[buf, HALO - 1:HALO - 1 + ts, 0:width]
        p2 = ubuf_ref[buf, HALO - 2:HALO - 2 + ts, 0:width]
        w = wconv_ref[:, cols]
        return w[0:1, :] * p2 + w[1:2, :] * p1 + w[2:3, :] * u + bconv_ref[:, cols]

    y = jnp.zeros((ts, D_MODEL), F32)
    for c0, width in FF_CHUNKS:
        u_val = conv(c0, width, 0)
        u_gate = conv(D_FF + c0, width, 1)
        act = (_gelu_tanh(u_gate) * u_val).astype(BF16)
        y = y + _dot(act, wdown_ref[c0:c0 + width, :])
    x2 = x1 + gt_f * y
    o_ref[0] = _rms(x2) * gfin_ref[...]


def _post(x, oa, ob, proj, mod, g_ffn, g_final, wbm, wbs, wout, wup, wconv, bconv, wdown):
    b, s, d = x.shape
    ts = POST_TS
    ga_blk = 6 * MIX_WIDTH // D_MODEL
    tok = lambda i, j: (i, j, 0)
    const2 = lambda i, j: (0, 0)
    resident = functools.partial(pl.BlockSpec, index_map=const2, pipeline_mode=pl.Buffered(1))
    return pl.pallas_call(
        _post_kernel,
        grid=(b, s // ts),
        in_specs=[pl.BlockSpec((1, ts, d), tok),
                  pl.BlockSpec((1, ts, MIX_WIDTH), tok),
                  pl.BlockSpec((1, ts, MIX_WIDTH), tok),
                  pl.BlockSpec((1, ts, d), lambda i, j: (i, j, ga_blk)),
                  pl.BlockSpec((1, ts, d), lambda i, j: (i, j, ga_blk + 1)),
                  pl.BlockSpec((1, 6, d), lambda i, j: (i, 0, 0)),
                  pl.BlockSpec((1, d), const2),
                  pl.BlockSpec((1, d), const2),
                  resident((MIX_WIDTH, d)),
                  resident((MIX_WIDTH, d)),
                  resident((d, d)),
                  resident((d, 2 * D_FF)),
                  pl.BlockSpec((3, 2 * D_FF), const2),
                  pl.BlockSpec((1, 2 * D_FF), const2),
                  resident((D_FF, d))],
        out_specs=pl.BlockSpec((1, ts, d), tok),
        out_shape=jax.ShapeDtypeStruct((b, s, d), x.dtype),
        scratch_shapes=[pltpu.VMEM((HALO, 2 * D_FF), F32),
                        pltpu.VMEM((2, HALO + ts, FF_SPLIT), F32)],
        compiler_params=pltpu.CompilerParams(dimension_semantics=("arbitrary", "arbitrary"),
                                             vmem_limit_bytes=VMEM_LIMIT),
        name="post",
    )(x, oa, ob, proj, proj, mod, g_ffn.reshape(1, d), g_final.reshape(1, d),
      wbm, wbs, wout, wup, wconv, bconv.reshape(1, 2 * D_FF), wdown)


@jax.jit
def kernel(x, c, w_ada, b_ada, g_mix, w_in, w_br_moba, w_br_sb, w_out, rel_bias, g_ffn, w_up,
           w_conv, b_conv, w_down, g_final):
    assert w_ada.shape[0] == 1, "the final rms_norm is fused into the single layer's last kernel"
    l = 0
    bias = _bias_tiles(rel_bias)
    mod = _ada(c, w_ada[l], b_ada[l]).reshape(x.shape[0], 6, D_MODEL)
    proj = _proj(x, mod, g_mix[l], w_in[l].astype(BF16))
    oa = _moba(proj, bias)
    ob = _sb(proj)
    return _post(x, oa, ob, proj, mod, g_ffn[l], g_final,
                 w_br_moba[l].astype(BF16), w_br_sb[l].astype(BF16), w_out[l].astype(BF16),
                 w_up[l].astype(BF16), w_conv[l], b_conv[l], w_down[l].astype(BF16))
```

```python
import functools
import math

import numpy as np
import jax
import jax.numpy as jnp
from jax import lax
from jax.experimental import pallas as pl
from jax.experimental.pallas import tpu as pltpu

D_MODEL = 1024
HEAD_DIM = 64
N_HEADS = 8
MIX_WIDTH = N_HEADS * HEAD_DIM
IN_WIDTH = 6 * MIX_WIDTH + 2 * D_MODEL
MOBA_BLOCK = 256
MOBA_TOPK = 3
D_FF = 2816
REL_BUCKETS = 32
REL_MAX_DIST = 128
NORM_EPS = 1e-6
NEG_INF = -1e30
LOG2E = math.log2(math.e)

LANES = 128
PAIR = LANES // HEAD_DIM
N_PAIRS = N_HEADS // PAIR
BLK = MOBA_BLOCK
VMEM_LIMIT = 56 * 1024 * 1024

F32 = jnp.float32
BF16 = jnp.bfloat16


def _dot(a, b):
    return jnp.dot(a, b, preferred_element_type=F32)


def _dot_nt(a, b):
    return lax.dot_general(a, b, (((1,), (1,)), ((), ())), preferred_element_type=F32)


def _rms(x):
    return x * lax.rsqrt(jnp.mean(x * x, axis=-1, keepdims=True) + NORM_EPS)


def _ada_kernel(c_ref, w_ref, b_ref, o_ref):
    c = c_ref[...]
    ca = c * jax.nn.sigmoid(c)
    o_ref[...] = jnp.dot(ca, w_ref[...], precision=lax.Precision.HIGHEST,
                         preferred_element_type=F32) + b_ref[...]


def _ada(c, w_ada, b_ada):
    b, d = c.shape
    n = w_ada.shape[1]
    tn = D_MODEL
    return pl.pallas_call(
        _ada_kernel,
        grid=(n // tn,),
        in_specs=[pl.BlockSpec((b, d), lambda j: (0, 0)),
                  pl.BlockSpec((d, tn), lambda j: (0, j)),
                  pl.BlockSpec((1, tn), lambda j: (0, j))],
        out_specs=pl.BlockSpec((b, tn), lambda j: (0, j)),
        out_shape=jax.ShapeDtypeStruct((b, n), F32),
        compiler_params=pltpu.CompilerParams(dimension_semantics=("arbitrary",),
                                             vmem_limit_bytes=VMEM_LIMIT),
        name="ada",
    )(c, w_ada, b_ada.reshape(1, n))


def _bucket_tiles():
    def bucket(rel):
        n = np.maximum(rel, 0)
        max_exact = REL_BUCKETS // 2
        n_f = np.maximum(n, max_exact).astype(np.float32)
        large = max_exact + (np.log(n_f / np.float32(max_exact)) / np.float32(math.log(REL_MAX_DIST / max_exact))
                             * np.float32(REL_BUCKETS - max_exact)).astype(np.int32)
        large = np.minimum(large, REL_BUCKETS - 1)
        return np.where(n < max_exact, n, large).astype(np.int32)
    qi = np.arange(BLK)[:, None]
    kj = np.arange(BLK)[None, :]
    t0 = np.where(qi >= kj, bucket(qi - kj), REL_BUCKETS)
    t1 = bucket(qi - kj + BLK)
    far = bucket(np.array([2 * BLK - (BLK - 1)]))
    assert int(far[0]) == REL_BUCKETS - 1
    return np.stack([t0, t1]).astype(np.int32)


def _bias_kernel(tab_ref, bkt_ref, o_ref):
    h = pl.program_id(0)
    for t in range(2):
        b = bkt_ref[t]
        acc = jnp.full((BLK, BLK), NEG_INF, F32)
        for k in range(REL_BUCKETS):
            acc = jnp.where(b == k, tab_ref[h, k], acc)
        o_ref[0, t] = acc
    o_ref[0, 2] = jnp.full((BLK, BLK), tab_ref[h, REL_BUCKETS - 1], F32)


def _bias_tiles(rel_bias):
    bkt = jnp.asarray(_bucket_tiles())
    return pl.pallas_call(
        _bias_kernel,
        grid=(N_HEADS,),
        in_specs=[pl.BlockSpec(memory_space=pltpu.SMEM),
                  pl.BlockSpec((2, BLK, BLK), lambda h: (0, 0, 0))],
        out_specs=pl.BlockSpec((1, 3, BLK, BLK), lambda h: (h, 0, 0, 0)),
        out_shape=jax.ShapeDtypeStruct((N_HEADS, 3, BLK, BLK), F32),
        compiler_params=pltpu.CompilerParams(dimension_semantics=("arbitrary",)),
        name="bias_tiles",
    )(rel_bias, bkt)


PROJ_TS = 256
PROJ_NC = 512


def _proj_kernel(x_ref, mod_ref, g_ref, w_ref, o_ref):
    x = x_ref[0]
    sh = mod_ref[0, 0:1, :]
    sc = mod_ref[0, 1:2, :]
    h = (_rms(x) * g_ref[...] * (1.0 + sc) + sh).astype(BF16)
    scale = HEAD_DIM ** -0.5
    for n0 in range(0, IN_WIDTH, PROJ_NC):
        r = _dot(h, w_ref[:, n0:n0 + PROJ_NC])
        if n0 == 0:
            r = r * scale
        elif n0 == 3 * MIX_WIDTH:
            r = r * (scale * LOG2E)
        elif n0 >= 6 * MIX_WIDTH:
            r = jax.nn.sigmoid(r)
        o_ref[0, :, n0:n0 + PROJ_NC] = r.astype(BF16)


def _proj(x, mod, g_mix, w_in_bf16):
    b, s, d = x.shape
    ts = PROJ_TS
    return pl.pallas_call(
        _proj_kernel,
        grid=(b, s // ts),
        in_specs=[pl.BlockSpec((1, ts, d), lambda i, j: (i, j, 0)),
                  pl.BlockSpec((1, 6, d), lambda i, j: (i, 0, 0)),
                  pl.BlockSpec((1, d), lambda i, j: (0, 0)),
                  pl.BlockSpec((d, IN_WIDTH), lambda i, j: (0, 0), pipeline_mode=pl.Buffered(1))],
        out_specs=pl.BlockSpec((1, ts, IN_WIDTH), lambda i, j: (i, j, 0)),
        out_shape=jax.ShapeDtypeStruct((b, s, IN_WIDTH), BF16),
        compiler_params=pltpu.CompilerParams(dimension_semantics=("arbitrary", "arbitrary"),
                                             vmem_limit_bytes=VMEM_LIMIT),
        name="proj",
    )(x, mod, g_mix.reshape(1, d), w_in_bf16)


ATTN_UNROLL = 4


def _head_lanes(shape, h):
    lane = lax.broadcasted_iota(jnp.int32, shape, len(shape) - 1)
    return (lane >= h * HEAD_DIM) & (lane < (h + 1) * HEAD_DIM)


def _rows(blk):
    return slice(blk * BLK, (blk + 1) * BLK)


def _run_chains(chains, step):
    states = [None] * len(chains)
    for k in range(max(len(c) for c in chains)):
        for ci, chain in enumerate(chains):
            if k < len(chain):
                states[ci] = step(chain[k], states[ci])
    return states


def _pair_chains(t, n_blk):
    qis = (t, n_blk - 1 - t)
    return qis, [[(qi, j, h) for j in range(qi, -1, -1)] for qi in qis for h in range(PAIR)]


def _store_pair(o_ref, qi, out0, out1):
    o_ref[0, _rows(qi), :] = jnp.where(_head_lanes((BLK, LANES), 0), out0, out1).astype(o_ref.dtype)


def _moba_prepare(q_ref, k_ref, qaug_ref, kaug_ref, pen_ref):
    seq = k_ref.shape[1]
    n_blk = seq // BLK
    k = k_ref[0]
    q = q_ref[0]
    sub = 16
    r = lax.broadcasted_iota(jnp.int32, (sub, seq), 0)
    c = lax.broadcasted_iota(jnp.int32, (sub, seq), 1)
    avg = jnp.where(c // BLK == r, 1.0 / BLK, 0.0).astype(BF16)
    kmean = _dot(avg, k)
    row = lax.broadcasted_iota(jnp.int32, (seq, LANES), 0)
    lane = lax.broadcasted_iota(jnp.int32, (seq, LANES), 1)
    rr = lax.broadcasted_iota(jnp.int32, (BLK, BLK), 0)
    cc = lax.broadcasted_iota(jnp.int32, (BLK, BLK), 1)
    eye = jnp.where(rr == cc, 1.0, 0.0).astype(BF16)
    jidx = lax.broadcasted_iota(jnp.int32, (8, seq), 0)
    qblk = lax.broadcasted_iota(jnp.int32, (8, seq), 1) // BLK
    for h in range(PAIR):
        ind0 = (1 - h) * HEAD_DIM
        ind = (lane - ind0) == (row // BLK)
        kaug_ref[h] = jnp.where(_head_lanes((seq, LANES), h), k, jnp.where(ind, 1.0, 0.0).astype(BF16))
        kmh = jnp.where(_head_lanes((sub, LANES), h), kmean, 0.0)
        hi = kmh.astype(BF16)
        lo = (kmh - hi.astype(F32)).astype(BF16)
        g8 = (_dot_nt(hi, q) + _dot_nt(lo, q))[0:8, :]
        cnt = jnp.zeros((8, seq), jnp.int32)
        for jp in range(n_blk):
            rowv = g8[jp:jp + 1, :]
            beats = (rowv > g8) | ((rowv == g8) & (jp < jidx))
            cnt = cnt + jnp.where(beats & (jp < qblk), 1, 0)
        sel = (jidx == qblk) | ((jidx < qblk) & (cnt < MOBA_TOPK))
        pen_ref[...] = jnp.zeros(pen_ref.shape, F32)
        pen_ref[ind0:ind0 + 8, :] = jnp.where(sel, 0.0, NEG_INF)
        for b in range(n_blk):
            pen_cols = _dot_nt(eye, pen_ref[:, _rows(b)].astype(BF16))
            qaug_ref[h, _rows(b), :] = jnp.where(_head_lanes((BLK, LANES), h), q[_rows(b), :],
                                                 pen_cols.astype(BF16))


def _moba_pair(t, n_blk, v_ref, bias_ref, o_ref, qaug_ref, kaug_ref):
    def step(args, st):
        qi, j, h = args
        s = _dot_nt(qaug_ref[h, _rows(qi), :], kaug_ref[h, _rows(j), :]) + bias_ref[h, min(qi - j, 2)]
        m_blk = jnp.max(s, axis=-1, keepdims=True)
        vj = v_ref[0, _rows(j), :]
        if st is None:
            p = jnp.exp(s - m_blk)
            return m_blk, jnp.sum(p, axis=-1, keepdims=True), _dot(p.astype(BF16), vj)
        m, l, acc = st
        m_new = jnp.maximum(m, m_blk)
        alpha = jnp.exp(m - m_new)
        p = jnp.exp(s - m_new)
        return (m_new, alpha * l + jnp.sum(p, axis=-1, keepdims=True),
                alpha * acc + _dot(p.astype(BF16), vj))

    qis, chains = _pair_chains(t, n_blk)
    states = _run_chains(chains, step)
    for n, qi in enumerate(qis):
        (_, l0, a0), (_, l1, a1) = states[PAIR * n], states[PAIR * n + 1]
        _store_pair(o_ref, qi, a0 / l0, a1 / l1)


def _moba_kernel(q_ref, k_ref, v_ref, bias_ref, o_ref, qaug_ref, kaug_ref, pen_ref):
    t = pl.program_id(2)
    n_blk = k_ref.shape[1] // BLK

    @pl.when(t == 0)
    def _prepare():
        _moba_prepare(q_ref, k_ref, qaug_ref, kaug_ref, pen_ref)

    for tt in range(n_blk // 2):
        pl.when(t == tt)(functools.partial(_moba_pair, tt, n_blk, v_ref, bias_ref, o_ref, qaug_ref, kaug_ref))


def _moba(proj, bias):
    b, s, _ = proj.shape
    n_blk = s // BLK
    assert s % BLK == 0 and n_blk % 2 == 0 and n_blk <= 8
    full = lambda off: pl.BlockSpec((1, s, LANES), lambda bi, p, t: (bi, 0, off + p))
    return pl.pallas_call(
        _moba_kernel,
        grid=(b, N_PAIRS, n_blk // 2),
        in_specs=[full(0), full(MIX_WIDTH // LANES), full(2 * MIX_WIDTH // LANES),
                  pl.BlockSpec((PAIR, 3, BLK, BLK), lambda bi, p, t: (p, 0, 0, 0))],
        out_specs=pl.BlockSpec((1, s, LANES), lambda bi, p, t: (bi, 0, p)),
        out_shape=jax.ShapeDtypeStruct((b, s, MIX_WIDTH), BF16),
        scratch_shapes=[pltpu.VMEM((PAIR, s, LANES), BF16),
                        pltpu.VMEM((PAIR, s, LANES), BF16),
                        pltpu.VMEM((LANES, s), F32)],
        compiler_params=pltpu.CompilerParams(dimension_semantics=("arbitrary",) * 3,
                                             vmem_limit_bytes=VMEM_LIMIT),
        name="moba",
    )(proj, proj, proj, bias)


SB_MASK = -1e9


def _sb_mask_tiles():
    r = np.arange(BLK)[:, None]
    c = np.arange(BLK)[None, :]
    tri = np.where(c < r, 0.0, SB_MASK)
    return np.stack([np.zeros((BLK, PAIR * BLK)), np.tile(tri, (1, PAIR))]).astype(np.float32)


def _sb_kernel(q_ref, k_ref, v_ref, mask_ref, o_ref, ks_ref, vs_ref, lk_ref, lb_ref, a_ref, acc_ref):
    seq = k_ref.shape[1]
    n_blk = seq // BLK
    n_tiles = n_blk * (n_blk + 1) // 2
    wide = PAIR * BLK
    rr = lax.broadcasted_iota(jnp.int32, (BLK, BLK), 0)
    cc = lax.broadcasted_iota(jnp.int32, (BLK, BLK), 1)
    upper = jnp.where(rr > cc, 1.0, 0.0).astype(BF16)
    sign_bit = jnp.uint32(0x80000000)

    for h in range(PAIR):
        in_head = _head_lanes((BLK, LANES), h)
        for blk in range(n_blk):
            ks_ref[blk, h * BLK:(h + 1) * BLK, :] = jnp.where(in_head, k_ref[0, _rows(blk), :], jnp.zeros((), BF16))
            vs_ref[blk, h * BLK:(h + 1) * BLK, :] = jnp.where(in_head, v_ref[0, _rows(blk), :], jnp.zeros((), BF16))

    def rows(blk):
        return pl.ds(pl.multiple_of(blk * BLK, BLK), BLK)

    def cols(h):
        return slice(h * BLK, (h + 1) * BLK)

    def next_tile(t):
        qi, j = t
        last = j == 0
        return jnp.where(last, qi + 1, qi), jnp.where(last, qi + 1, j - 1)

    def stage1(t, slot, run):
        qi, j = t
        diag = qi == j
        z = _dot_nt(q_ref[0, rows(qi), :], ks_ref[j]) + mask_ref[jnp.where(diag, 1, 0)]
        neg_abs = lax.bitcast_convert_type(lax.bitcast_convert_type(z, jnp.uint32) | sign_bit, F32)
        sp = jnp.log(1.0 + jnp.exp2(neg_abs)) * LOG2E
        log_beta = jnp.minimum(z, 0.0) - sp
        log_keep = log_beta - z
        lk_ref[slot] = log_keep.astype(BF16)
        lb_ref[slot] = log_beta
        keep = jnp.where(diag, 0.0, 1.0)
        through, before = [], []
        for h in range(PAIR):
            prev = run[h] * keep
            before.append(prev)
            through.append(prev + jnp.sum(log_keep[:, cols(h)], axis=-1, keepdims=True))
        return through, before

    def stage2(slot, before):
        for h in range(PAIR):
            later = _dot(lk_ref[slot, :, cols(h)], upper)
            a_ref[slot, :, cols(h)] = jnp.exp2(lb_ref[slot, :, cols(h)] + before[h] + later).astype(BF16)

    def stage3(t, slot):
        qi, j = t
        pv = _dot(a_ref[slot], vs_ref[j])
        acc = jnp.where(qi == j, pv, acc_ref[...] + pv)
        acc_ref[...] = acc
        o_ref[0, rows(qi), :] = acc.astype(o_ref.dtype)

    zero = jnp.zeros((BLK, 1), F32)
    t0 = (jnp.int32(0), jnp.int32(0))
    t1 = next_tile(t0)
    run, before0 = stage1(t0, 0, [zero] * PAIR)
    run, before1 = stage1(t1, 1, run)
    stage2(0, before0)

    def pipeline_step(state, slot):
        t_cur, t_m1, t_m2, run, before_m1 = state
        run, before_cur = stage1(t_cur, slot, run)
        stage2(1 - slot, before_m1)
        stage3(t_m2, slot)
        return next_tile(t_cur), t_cur, t_m1, run, before_cur

    def body(_, state):
        for u in range(ATTN_UNROLL):
            state = pipeline_step(state, u % 2)
        return state

    assert ATTN_UNROLL % 2 == 0 and (n_tiles - 4) % ATTN_UNROLL == 0
    state = lax.fori_loop(0, (n_tiles - 4) // ATTN_UNROLL, body, (next_tile(t1), t1, t0, run, before1))
    state = pipeline_step(state, 0)
    _, t_m1, t_m2, _, before_m1 = pipeline_step(state, 1)
    stage2(1, before_m1)
    stage3(t_m2, 0)
    stage3(t_m1, 1)


def _sb(proj):
    b, s, _ = proj.shape
    assert s % BLK == 0
    base = 3 * MIX_WIDTH // LANES
    full = lambda off: pl.BlockSpec((1, s, LANES), lambda bi, p: (bi, 0, off + p))
    return pl.pallas_call(
        _sb_kernel,
        grid=(b, N_PAIRS),
        in_specs=[full(base), full(base + MIX_WIDTH // LANES), full(base + 2 * MIX_WIDTH // LANES),
                  pl.BlockSpec((2, BLK, PAIR * BLK), lambda bi, p: (0, 0, 0))],
        out_specs=pl.BlockSpec((1, s, LANES), lambda bi, p: (bi, 0, p)),
        out_shape=jax.ShapeDtypeStruct((b, s, MIX_WIDTH), BF16),
        scratch_shapes=[pltpu.VMEM((s // BLK, PAIR * BLK, LANES), BF16),
                        pltpu.VMEM((s // BLK, PAIR * BLK, LANES), BF16),
                        pltpu.VMEM((2, BLK, PAIR * BLK), BF16),
                        pltpu.VMEM((2, BLK, PAIR * BLK), F32),
                        pltpu.VMEM((2, BLK, PAIR * BLK), BF16),
                        pltpu.VMEM((BLK, LANES), F32)],
        compiler_params=pltpu.CompilerParams(dimension_semantics=("arbitrary",) * 2,
                                             vmem_limit_bytes=VMEM_LIMIT),
        name="sb",
    )(proj, proj, proj, jnp.asarray(_sb_mask_tiles()))


POST_TS = 512
MXU_TILE = 256
FF_SPLIT = (D_FF // MXU_TILE + 1) // 2 * MXU_TILE
FF_CHUNKS = ((0, FF_SPLIT), (FF_SPLIT, D_FF - FF_SPLIT))
HALO = 8


def _gelu_tanh(x):
    return 0.5 * x * (1.0 + jnp.tanh(math.sqrt(2.0 / math.pi) * (x + 0.044715 * (x * x * x))))


def _post_kernel(x_ref, oa_ref, ob_ref, ga_ref, gb_ref, mod_ref, gffn_ref, gfin_ref,
                 wbm_ref, wbs_ref, wout_ref, wup_ref, wconv_ref, bconv_ref, wdown_ref,
                 o_ref, halo_ref, ubuf_ref):
    ts = x_ref.shape[1]

    @pl.when(pl.program_id(1) == 0)
    def _zero_halo():
        halo_ref[...] = jnp.zeros_like(halo_ref)

    gt_m = mod_ref[0, 2:3, :]
    sh_f = mod_ref[0, 3:4, :]
    sc_f = mod_ref[0, 4:5, :]
    gt_f = mod_ref[0, 5:6, :]

    ya = _dot(oa_ref[0], wbm_ref[...])
    yb = _dot(ob_ref[0], wbs_ref[...])
    mix = (ga_ref[0].astype(F32) * ya + gb_ref[0].astype(F32) * yb).astype(BF16)
    x1 = x_ref[0] + gt_m * _dot(mix, wout_ref[...])
    h2 = (_rms(x1) * gffn_ref[...] * (1.0 + sc_f) + sh_f).astype(BF16)

    def conv(c0, width, buf):
        cols = slice(c0, c0 + width)
        u = _dot(h2, wup_ref[:, cols])
        ubuf_ref[buf, 0:HALO, 0:width] = halo_ref[:, cols]
        ubuf_ref[buf, HALO:HALO + ts, 0:width] = u
        halo_ref[:, cols] = u[ts - HALO:ts, :]
        p1 = ubuf_ref[buf, HALO - 1:HALO - 1 + ts, 0:width]
        p2 = ubuf_ref[buf, HALO - 2:HALO - 2 + ts, 0:width]
        w = wconv_ref[:, cols]
        return w[0:1, :] * p2 + w[1:2, :] * p1 + w[2:3, :] * u + bconv_ref[:, cols]

    y = jnp.zeros((ts, D_MODEL), F32)
    for c0, width in FF_CHUNKS:
        u_val = conv(c0, width, 0)
        u_gate = conv(D_FF + c0, width, 1)
        act = (_gelu_tanh(u_gate) * u_val).astype(BF16)
        y = y + _dot(act, wdown_ref[c0:c0 + width, :])
    x2 = x1 + gt_f * y
    o_ref[0] = _rms(x2) * gfin_ref[...]


def _post(x, oa, ob, proj, mod, g_ffn, g_final, wbm, wbs, wout, wup, wconv, bconv, wdown):
    b, s, d = x.shape
    ts = POST_TS
    ga_blk = 6 * MIX_WIDTH // D_MODEL
    tok = lambda i, j: (i, j, 0)
    const2 = lambda i, j: (0, 0)
    resident = functools.partial(pl.BlockSpec, index_map=const2, pipeline_mode=pl.Buffered(1))
    return pl.pallas_call(
        _post_kernel,
        grid=(b, s // ts),
        in_specs=[pl.BlockSpec((1, ts, d), tok),
                  pl.BlockSpec((1, ts, MIX_WIDTH), tok),
                  pl.BlockSpec((1, ts, MIX_WIDTH), tok),
                  pl.BlockSpec((1, ts, d), lambda i, j: (i, j, ga_blk)),
                  pl.BlockSpec((1, ts, d), lambda i, j: (i, j, ga_blk + 1)),
                  pl.BlockSpec((1, 6, d), lambda i, j: (i, 0, 0)),
                  pl.BlockSpec((1, d), const2),
                  pl.BlockSpec((1, d), const2),
                  resident((MIX_WIDTH, d)),
                  resident((MIX_WIDTH, d)),
                  resident((d, d)),
                  resident((d, 2 * D_FF)),
                  pl.BlockSpec((3, 2 * D_FF), const2),
                  pl.BlockSpec((1, 2 * D_FF), const2),
                  resident((D_FF, d))],
        out_specs=pl.BlockSpec((1, ts, d), tok),
        out_shape=jax.ShapeDtypeStruct((b, s, d), x.dtype),
        scratch_shapes=[pltpu.VMEM((HALO, 2 * D_FF), F32),
                        pltpu.VMEM((2, HALO + ts, FF_SPLIT), F32)],
        compiler_params=pltpu.CompilerParams(dimension_semantics=("arbitrary", "arbitrary"),
                                             vmem_limit_bytes=VMEM_LIMIT),
        name="post",
    )(x, oa, ob, proj, proj, mod, g_ffn.reshape(1, d), g_final.reshape(1, d),
      wbm, wbs, wout, wup, wconv, bconv.reshape(1, 2 * D_FF), wdown)


@jax.jit
def kernel(x, c, w_ada, b_ada, g_mix, w_in, w_br_moba, w_br_sb, w_out, rel_bias, g_ffn, w_up,
           w_conv, b_conv, w_down, g_final):
    assert w_ada.shape[0] == 1, "the final rms_norm is fused into the single layer's last kernel"
    l = 0
    bias = _bias_tiles(rel_bias)
    mod = _ada(c, w_ada[l], b_ada[l]).reshape(x.shape[0], 6, D_MODEL)
    proj = _proj(x, mod, g_mix[l], w_in[l].astype(BF16))
    oa = _moba(proj, bias)
    ob = _sb(proj)
    return _post(x, oa, ob, proj, mod, g_ffn[l], g_final,
                 w_br_moba[l].astype(BF16), w_br_sb[l].astype(BF16), w_out[l].astype(BF16),
                 w_up[l].astype(BF16), w_conv[l], b_conv[l], w_down[l].astype(BF16))
```

```python
import functools
import math

import numpy as np
import jax
import jax.numpy as jnp
from jax import lax
from jax.experimental import pallas as pl
from jax.experimental.pallas import tpu as pltpu

D_MODEL = 1024
HEAD_DIM = 64
N_HEADS = 8
MIX_WIDTH = N_HEADS * HEAD_DIM
IN_WIDTH = 6 * MIX_WIDTH + 2 * D_MODEL
MOBA_BLOCK = 256
MOBA_TOPK = 3
D_FF = 2816
REL_BUCKETS = 32
REL_MAX_DIST = 128
NORM_EPS = 1e-6
NEG_INF = -1e30
LOG2E = math.log2(math.e)

LANES = 128
PAIR = LANES // HEAD_DIM
N_PAIRS = N_HEADS // PAIR
BLK = MOBA_BLOCK
VMEM_LIMIT = 56 * 1024 * 1024

F32 = jnp.float32
BF16 = jnp.bfloat16


def _dot(a, b):
    return jnp.dot(a, b, preferred_element_type=F32)


def _dot_nt(a, b):
    return lax.dot_general(a, b, (((1,), (1,)), ((), ())), preferred_element_type=F32)


def _rms(x):
    return x * lax.rsqrt(jnp.mean(x * x, axis=-1, keepdims=True) + NORM_EPS)


def _ada_kernel(c_ref, w_ref, b_ref, o_ref):
    c = c_ref[...]
    ca = c * jax.nn.sigmoid(c)
    o_ref[...] = jnp.dot(ca, w_ref[...], precision=lax.Precision.HIGHEST,
                         preferred_element_type=F32) + b_ref[...]


def _ada(c, w_ada, b_ada):
    b, d = c.shape
    n = w_ada.shape[1]
    tn = D_MODEL
    return pl.pallas_call(
        _ada_kernel,
        grid=(n // tn,),
        in_specs=[pl.BlockSpec((b, d), lambda j: (0, 0)),
                  pl.BlockSpec((d, tn), lambda j: (0, j)),
                  pl.BlockSpec((1, tn), lambda j: (0, j))],
        out_specs=pl.BlockSpec((b, tn), lambda j: (0, j)),
        out_shape=jax.ShapeDtypeStruct((b, n), F32),
        compiler_params=pltpu.CompilerParams(dimension_semantics=("arbitrary",),
                                             vmem_limit_bytes=VMEM_LIMIT),
        name="ada",
    )(c, w_ada, b_ada.reshape(1, n))


def _bucket_tiles():
    def bucket(rel):
        n = np.maximum(rel, 0)
        max_exact = REL_BUCKETS // 2
        n_f = np.maximum(n, max_exact).astype(np.float32)
        large = max_exact + (np.log(n_f / np.float32(max_exact)) / np.float32(math.log(REL_MAX_DIST / max_exact))
                             * np.float32(REL_BUCKETS - max_exact)).astype(np.int32)
        large = np.minimum(large, REL_BUCKETS - 1)
        return np.where(n < max_exact, n, large).astype(np.int32)
    qi = np.arange(BLK)[:, None]
    kj = np.arange(BLK)[None, :]
    t0 = np.where(qi >= kj, bucket(qi - kj), REL_BUCKETS)
    t1 = bucket(qi - kj + BLK)
    far = bucket(np.array([2 * BLK - (BLK - 1)]))
    assert int(far[0]) == REL_BUCKETS - 1
    return np.stack([t0, t1]).astype(np.int32)


def _bias_kernel(tab_ref, bkt_ref, o_ref):
    h = pl.program_id(0)
    for t in range(2):
        b = bkt_ref[t]
        acc = jnp.full((BLK, BLK), NEG_INF, F32)
        for k in range(REL_BUCKETS):
            acc = jnp.where(b == k, tab_ref[h, k] * LOG2E, acc)
        o_ref[0, t] = acc
    o_ref[0, 2] = jnp.full((BLK, BLK), tab_ref[h, REL_BUCKETS - 1] * LOG2E, F32)


def _bias_tiles(rel_bias):
    bkt = jnp.asarray(_bucket_tiles())
    return pl.pallas_call(
        _bias_kernel,
        grid=(N_HEADS,),
        in_specs=[pl.BlockSpec(memory_space=pltpu.SMEM),
                  pl.BlockSpec((2, BLK, BLK), lambda h: (0, 0, 0))],
        out_specs=pl.BlockSpec((1, 3, BLK, BLK), lambda h: (h, 0, 0, 0)),
        out_shape=jax.ShapeDtypeStruct((N_HEADS, 3, BLK, BLK), F32),
        compiler_params=pltpu.CompilerParams(dimension_semantics=("arbitrary",)),
        name="bias_tiles",
    )(rel_bias, bkt)


PROJ_TS = 256
PROJ_NC = 512


def _proj_kernel(x_ref, mod_ref, g_ref, w_ref, o_ref):
    x = x_ref[0]
    sh = mod_ref[0, 0:1, :]
    sc = mod_ref[0, 1:2, :]
    h = (_rms(x) * g_ref[...] * (1.0 + sc) + sh).astype(BF16)
    scale = HEAD_DIM ** -0.5
    for n0 in range(0, IN_WIDTH, PROJ_NC):
        r = _dot(h, w_ref[:, n0:n0 + PROJ_NC])
        if n0 in (0, 3 * MIX_WIDTH):
            r = r * (scale * LOG2E)
        elif n0 >= 6 * MIX_WIDTH:
            r = jax.nn.sigmoid(r)
        o_ref[0, :, n0:n0 + PROJ_NC] = r.astype(BF16)


def _proj(x, mod, g_mix, w_in_bf16):
    b, s, d = x.shape
    ts = PROJ_TS
    return pl.pallas_call(
        _proj_kernel,
        grid=(b, s // ts),
        in_specs=[pl.BlockSpec((1, ts, d), lambda i, j: (i, j, 0)),
                  pl.BlockSpec((1, 6, d), lambda i, j: (i, 0, 0)),
                  pl.BlockSpec((1, d), lambda i, j: (0, 0)),
                  pl.BlockSpec((d, IN_WIDTH), lambda i, j: (0, 0), pipeline_mode=pl.Buffered(1))],
        out_specs=pl.BlockSpec((1, ts, IN_WIDTH), lambda i, j: (i, j, 0)),
        out_shape=jax.ShapeDtypeStruct((b, s, IN_WIDTH), BF16),
        compiler_params=pltpu.CompilerParams(dimension_semantics=("arbitrary", "arbitrary"),
                                             vmem_limit_bytes=VMEM_LIMIT),
        name="proj",
    )(x, mod, g_mix.reshape(1, d), w_in_bf16)


ATTN_UNROLL = 8


def _head_lanes(shape, h):
    lane = lax.broadcasted_iota(jnp.int32, shape, len(shape) - 1)
    return (lane >= h * HEAD_DIM) & (lane < (h + 1) * HEAD_DIM)


def _rows(blk):
    return slice(blk * BLK, (blk + 1) * BLK)


def _run_chains(chains, step):
    states = [None] * len(chains)
    for k in range(max(len(c) for c in chains)):
        for ci, chain in enumerate(chains):
            if k < len(chain):
                states[ci] = step(chain[k], states[ci])
    return states


def _pair_chains(t, n_blk):
    qis = (t, n_blk - 1 - t)
    return qis, [[(qi, j, h) for j in range(qi, -1, -1)] for qi in qis for h in range(PAIR)]


def _store_pair(o_ref, qi, out0, out1):
    o_ref[0, _rows(qi), :] = jnp.where(_head_lanes((BLK, LANES), 0), out0, out1).astype(o_ref.dtype)


def _moba_prepare(q_ref, k_ref, qaug_ref, kaug_ref, pen_ref):
    seq = k_ref.shape[1]
    n_blk = seq // BLK
    k = k_ref[0]
    q = q_ref[0]
    sub = 16
    r = lax.broadcasted_iota(jnp.int32, (sub, seq), 0)
    c = lax.broadcasted_iota(jnp.int32, (sub, seq), 1)
    avg = jnp.where(c // BLK == r, 1.0 / BLK, 0.0).astype(BF16)
    kmean = _dot(avg, k)
    row = lax.broadcasted_iota(jnp.int32, (seq, LANES), 0)
    lane = lax.broadcasted_iota(jnp.int32, (seq, LANES), 1)
    rr = lax.broadcasted_iota(jnp.int32, (BLK, BLK), 0)
    cc = lax.broadcasted_iota(jnp.int32, (BLK, BLK), 1)
    eye = jnp.where(rr == cc, 1.0, 0.0).astype(BF16)
    jidx = lax.broadcasted_iota(jnp.int32, (8, seq), 0)
    qblk = lax.broadcasted_iota(jnp.int32, (8, seq), 1) // BLK
    for h in range(PAIR):
        ind0 = (1 - h) * HEAD_DIM
        ind = (lane - ind0) == (row // BLK)
        kaug_ref[h] = jnp.where(_head_lanes((seq, LANES), h), k, jnp.where(ind, 1.0, 0.0).astype(BF16))
        kmh = jnp.where(_head_lanes((sub, LANES), h), kmean, 0.0)
        hi = kmh.astype(BF16)
        lo = (kmh - hi.astype(F32)).astype(BF16)
        g8 = (_dot_nt(hi, q) + _dot_nt(lo, q))[0:8, :]
        cnt = jnp.zeros((8, seq), jnp.int32)
        for jp in range(n_blk):
            rowv = g8[jp:jp + 1, :]
            beats = (rowv > g8) | ((rowv == g8) & (jp < jidx))
            cnt = cnt + jnp.where(beats & (jp < qblk), 1, 0)
        sel = (jidx == qblk) | ((jidx < qblk) & (cnt < MOBA_TOPK))
        pen_ref[...] = jnp.zeros(pen_ref.shape, F32)
        pen_ref[ind0:ind0 + 8, :] = jnp.where(sel, 0.0, NEG_INF)
        for b in range(n_blk):
            pen_cols = _dot_nt(eye, pen_ref[:, _rows(b)].astype(BF16))
            qaug_ref[h, _rows(b), :] = jnp.where(_head_lanes((BLK, LANES), h), q[_rows(b), :],
                                                 pen_cols.astype(BF16))


def _moba_pair(t, n_blk, v_ref, bias_ref, o_ref, qaug_ref, kaug_ref):
    def step(args, st):
        qi, j, h = args
        s = _dot_nt(qaug_ref[h, _rows(qi), :], kaug_ref[h, _rows(j), :]) + bias_ref[h, min(qi - j, 2)]
        m_blk = jnp.max(s, axis=-1, keepdims=True)
        vj = v_ref[0, _rows(j), :]
        if st is None:
            p = jnp.exp2(s - m_blk)
            return m_blk, jnp.sum(p, axis=-1, keepdims=True), _dot(p.astype(BF16), vj)
        m, l, acc = st
        m_new = jnp.maximum(m, m_blk)
        alpha = jnp.exp2(m - m_new)
        p = jnp.exp2(s - m_new)
        return (m_new, alpha * l + jnp.sum(p, axis=-1, keepdims=True),
                alpha * acc + _dot(p.astype(BF16), vj))

    qis, chains = _pair_chains(t, n_blk)
    states = _run_chains(chains, step)
    for n, qi in enumerate(qis):
        (_, l0, a0), (_, l1, a1) = states[PAIR * n], states[PAIR * n + 1]
        _store_pair(o_ref, qi, a0 / l0, a1 / l1)


def _moba_kernel(q_ref, k_ref, v_ref, bias_ref, o_ref, qaug_ref, kaug_ref, pen_ref):
    t = pl.program_id(2)
    n_blk = k_ref.shape[1] // BLK

    @pl.when(t == 0)
    def _prepare():
        _moba_prepare(q_ref, k_ref, qaug_ref, kaug_ref, pen_ref)

    for tt in range(n_blk // 2):
        pl.when(t == tt)(functools.partial(_moba_pair, tt, n_blk, v_ref, bias_ref, o_ref, qaug_ref, kaug_ref))


def _moba(proj, bias):
    b, s, _ = proj.shape
    n_blk = s // BLK
    assert s % BLK == 0 and n_blk % 2 == 0 and n_blk <= 8
    full = lambda off: pl.BlockSpec((1, s, LANES), lambda bi, p, t: (bi, 0, off + p))
    return pl.pallas_call(
        _moba_kernel,
        grid=(b, N_PAIRS, n_blk // 2),
        in_specs=[full(0), full(MIX_WIDTH // LANES), full(2 * MIX_WIDTH // LANES),
                  pl.BlockSpec((PAIR, 3, BLK, BLK), lambda bi, p, t: (p, 0, 0, 0))],
        out_specs=pl.BlockSpec((1, s, LANES), lambda bi, p, t: (bi, 0, p)),
        out_shape=jax.ShapeDtypeStruct((b, s, MIX_WIDTH), BF16),
        scratch_shapes=[pltpu.VMEM((PAIR, s, LANES), BF16),
                        pltpu.VMEM((PAIR, s, LANES), BF16),
                        pltpu.VMEM((LANES, s), F32)],
        compiler_params=pltpu.CompilerParams(dimension_semantics=("arbitrary",) * 3,
                                             vmem_limit_bytes=VMEM_LIMIT),
        name="moba",
    )(proj, proj, proj, bias)


SB_MASK = -1e9


def _sb_mask_tiles():
    r = np.arange(BLK)[:, None]
    c = np.arange(BLK)[None, :]
    tri = np.where(c < r, 0.0, SB_MASK)
    return np.stack([np.zeros((BLK, PAIR * BLK)), np.tile(tri, (1, PAIR))]).astype(np.float32)


def _sb_kernel(q_ref, k_ref, v_ref, mask_ref, o_ref, ks_ref, vs_ref, lk_ref, lb_ref, a_ref, acc_ref):
    seq = k_ref.shape[1]
    n_blk = seq // BLK
    n_tiles = n_blk * (n_blk + 1) // 2
    wide = PAIR * BLK
    rr = lax.broadcasted_iota(jnp.int32, (BLK, BLK), 0)
    cc = lax.broadcasted_iota(jnp.int32, (BLK, BLK), 1)
    upper = jnp.where(rr > cc, 1.0, 0.0).astype(BF16)
    sign_bit = jnp.uint32(0x80000000)

    for h in range(PAIR):
        in_head = _head_lanes((BLK, LANES), h)
        for blk in range(n_blk):
            ks_ref[blk, h * BLK:(h + 1) * BLK, :] = jnp.where(in_head, k_ref[0, _rows(blk), :], jnp.zeros((), BF16))
            vs_ref[blk, h * BLK:(h + 1) * BLK, :] = jnp.where(in_head, v_ref[0, _rows(blk), :], jnp.zeros((), BF16))

    def rows(blk):
        return pl.ds(pl.multiple_of(blk * BLK, BLK), BLK)

    def cols(h):
        return slice(h * BLK, (h + 1) * BLK)

    def next_tile(t):
        qi, j = t
        last = j == 0
        return jnp.where(last, qi + 1, qi), jnp.where(last, qi + 1, j - 1)

    def stage1(t, slot, run):
        qi, j = t
        diag = qi == j
        z = _dot_nt(q_ref[0, rows(qi), :], ks_ref[j]) + mask_ref[jnp.where(diag, 1, 0)]
        neg_abs = lax.bitcast_convert_type(lax.bitcast_convert_type(z, jnp.uint32) | sign_bit, F32)
        sp = jnp.log(1.0 + jnp.exp2(neg_abs)) * LOG2E
        log_beta = jnp.minimum(z, 0.0) - sp
        log_keep = log_beta - z
        lk_ref[slot] = log_keep.astype(BF16)
        lb_ref[slot] = log_beta
        keep = jnp.where(diag, 0.0, 1.0)
        through, before = [], []
        for h in range(PAIR):
            prev = run[h] * keep
            before.append(prev)
            through.append(prev + jnp.sum(log_keep[:, cols(h)], axis=-1, keepdims=True))
        return through, before

    def stage2(slot, before):
        for h in range(PAIR):
            later = _dot(lk_ref[slot, :, cols(h)], upper)
            a_ref[slot, :, cols(h)] = jnp.exp2(lb_ref[slot, :, cols(h)] + before[h] + later).astype(BF16)

    def stage3(t, slot):
        qi, j = t
        pv = _dot(a_ref[slot], vs_ref[j])
        acc = jnp.where(qi == j, pv, acc_ref[...] + pv)
        acc_ref[...] = acc
        o_ref[0, rows(qi), :] = acc.astype(o_ref.dtype)

    zero = jnp.zeros((BLK, 1), F32)
    t0 = (jnp.int32(0), jnp.int32(0))
    t1 = next_tile(t0)
    run, before0 = stage1(t0, 0, [zero] * PAIR)
    run, before1 = stage1(t1, 1, run)
    stage2(0, before0)

    def pipeline_step(state, slot):
        t_cur, t_m1, t_m2, run, before_m1 = state
        run, before_cur = stage1(t_cur, slot, run)
        stage2(1 - slot, before_m1)
        stage3(t_m2, slot)
        return next_tile(t_cur), t_cur, t_m1, run, before_cur

    def body(_, state):
        for u in range(ATTN_UNROLL):
            state = pipeline_step(state, u % 2)
        return state

    assert ATTN_UNROLL % 2 == 0 and (n_tiles - 4) % ATTN_UNROLL == 0
    state = lax.fori_loop(0, (n_tiles - 4) // ATTN_UNROLL, body, (next_tile(t1), t1, t0, run, before1))
    state = pipeline_step(state, 0)
    _, t_m1, t_m2, _, before_m1 = pipeline_step(state, 1)
    stage2(1, before_m1)
    stage3(t_m2, 0)
    stage3(t_m1, 1)


def _sb(proj):
    b, s, _ = proj.shape
    assert s % BLK == 0
    base = 3 * MIX_WIDTH // LANES
    full = lambda off: pl.BlockSpec((1, s, LANES), lambda bi, p: (bi, 0, off + p))
    return pl.pallas_call(
        _sb_kernel,
        grid=(b, N_PAIRS),
        in_specs=[full(base), full(base + MIX_WIDTH // LANES), full(base + 2 * MIX_WIDTH // LANES),
                  pl.BlockSpec((2, BLK, PAIR * BLK), lambda bi, p: (0, 0, 0))],
        out_specs=pl.BlockSpec((1, s, LANES), lambda bi, p: (bi, 0, p)),
        out_shape=jax.ShapeDtypeStruct((b, s, MIX_WIDTH), BF16),
        scratch_shapes=[pltpu.VMEM((s // BLK, PAIR * BLK, LANES), BF16),
                        pltpu.VMEM((s // BLK, PAIR * BLK, LANES), BF16),
                        pltpu.VMEM((2, BLK, PAIR * BLK), BF16),
                        pltpu.VMEM((2, BLK, PAIR * BLK), F32),
                        pltpu.VMEM((2, BLK, PAIR * BLK), BF16),
                        pltpu.VMEM((BLK, LANES), F32)],
        compiler_params=pltpu.CompilerParams(dimension_semantics=("arbitrary",) * 2,
                                             vmem_limit_bytes=VMEM_LIMIT),
        name="sb",
    )(proj, proj, proj, jnp.asarray(_sb_mask_tiles()))


POST_TS = 512
MXU_TILE = 256
FF_SPLIT = (D_FF // MXU_TILE + 1) // 2 * MXU_TILE
FF_CHUNKS = ((0, FF_SPLIT), (FF_SPLIT, D_FF - FF_SPLIT))
HALO = 8


def _gelu_tanh(x):
    return 0.5 * x * (1.0 + jnp.tanh(math.sqrt(2.0 / math.pi) * (x + 0.044715 * (x * x * x))))


def _post_kernel(x_ref, oa_ref, ob_ref, ga_ref, gb_ref, mod_ref, gffn_ref, gfin_ref,
                 wbm_ref, wbs_ref, wout_ref, wup_ref, wconv_ref, bconv_ref, wdown_ref,
                 o_ref, halo_ref, ubuf_ref):
    ts = x_ref.shape[1]

    @pl.when(pl.program_id(1) == 0)
    def _zero_halo():
        halo_ref[...] = jnp.zeros_like(halo_ref)

    gt_m = mod_ref[0, 2:3, :]
    sh_f = mod_ref[0, 3:4, :]
    sc_f = mod_ref[0, 4:5, :]
    gt_f = mod_ref[0, 5:6, :]

    ya = _dot(oa_ref[0], wbm_ref[...])
    yb = _dot(ob_ref[0], wbs_ref[...])
    mix = (ga_ref[0].astype(F32) * ya + gb_ref[0].astype(F32) * yb).astype(BF16)
    x1 = x_ref[0] + gt_m * _dot(mix, wout_ref[...])
    h2 = (_rms(x1) * gffn_ref[...] * (1.0 + sc_f) + sh_f).astype(BF16)

    def conv(c0, width, buf):
        cols = slice(c0, c0 + width)
        u = _dot(h2, wup_ref[:, cols])
        ubuf_ref[buf, 0:HALO, 0:width] = halo_ref[:, cols]
        ubuf_ref[buf, HALO:HALO + ts, 0:width] = u
        halo_ref[:, cols] = u[ts - HALO:ts, :]
        p1 = ubuf_ref[buf, HALO - 1:HALO - 1 + ts, 0:width]
        p2 = ubuf_ref[buf, HALO - 2:HALO - 2 + ts, 0:width]
        w = wconv_ref[:, cols]
        return w[0:1, :] * p2 + w[1:2, :] * p1 + w[2:3, :] * u + bconv_ref[:, cols]

    y = jnp.zeros((ts, D_MODEL), F32)
    for c0, width in FF_CHUNKS:
        u_val = conv(c0, width, 0)
        u_gate = conv(D_FF + c0, width, 1)
        act = (_gelu_tanh(u_gate) * u_val).astype(BF16)
        y = y + _dot(act, wdown_ref[c0:c0 + width, :])
    x2 = x1 + gt_f * y
    o_ref[0] = _rms(x2) * gfin_ref[...]


def _post(x, oa, ob, proj, mod, g_ffn, g_final, wbm, wbs, wout, wup, wconv, bconv, wdown):
    b, s, d = x.shape
    ts = POST_TS
    ga_blk = 6 * MIX_WIDTH // D_MODEL
    tok = lambda i, j: (i, j, 0)
    const2 = lambda i, j: (0, 0)
    resident = functools.partial(pl.BlockSpec, index_map=const2, pipeline_mode=pl.Buffered(1))
    return pl.pallas_call(
        _post_kernel,
        grid=(b, s // ts),
        in_specs=[pl.BlockSpec((1, ts, d), tok),
                  pl.BlockSpec((1, ts, MIX_WIDTH), tok),
                  pl.BlockSpec((1, ts, MIX_WIDTH), tok),
                  pl.BlockSpec((1, ts, d), lambda i, j: (i, j, ga_blk)),
                  pl.BlockSpec((1, ts, d), lambda i, j: (i, j, ga_blk + 1)),
                  pl.BlockSpec((1, 6, d), lambda i, j: (i, 0, 0)),
                  pl.BlockSpec((1, d), const2),
                  pl.BlockSpec((1, d), const2),
                  resident((MIX_WIDTH, d)),
                  resident((MIX_WIDTH, d)),
                  resident((d, d)),
                  resident((d, 2 * D_FF)),
                  pl.BlockSpec((3, 2 * D_FF), const2),
                  pl.BlockSpec((1, 2 * D_FF), const2),
                  resident((D_FF, d))],
        out_specs=pl.BlockSpec((1, ts, d), tok),
        out_shape=jax.ShapeDtypeStruct((b, s, d), x.dtype),
        scratch_shapes=[pltpu.VMEM((HALO, 2 * D_FF), F32),
                        pltpu.VMEM((2, HALO + ts, FF_SPLIT), F32)],
        compiler_params=pltpu.CompilerParams(dimension_semantics=("arbitrary", "arbitrary"),
                                             vmem_limit_bytes=VMEM_LIMIT),
        name="post",
    )(x, oa, ob, proj, proj, mod, g_ffn.reshape(1, d), g_final.reshape(1, d),
      wbm, wbs, wout, wup, wconv, bconv.reshape(1, 2 * D_FF), wdown)


@jax.jit
def kernel(x, c, w_ada, b_ada, g_mix, w_in, w_br_moba, w_br_sb, w_out, rel_bias, g_ffn, w_up,
           w_conv, b_conv, w_down, g_final):
    assert w_ada.shape[0] == 1, "the final rms_norm is fused into the single layer's last kernel"
    l = 0
    bias = _bias_tiles(rel_bias)
    mod = _ada(c, w_ada[l], b_ada[l]).reshape(x.shape[0], 6, D_MODEL)
    proj = _proj(x, mod, g_mix[l], w_in[l].astype(BF16))
    oa = _moba(proj, bias)
    ob = _sb(proj)
    return _post(x, oa, ob, proj, mod, g_ffn[l], g_final,
                 w_br_moba[l].astype(BF16), w_br_sb[l].astype(BF16), w_out[l].astype(BF16),
                 w_up[l].astype(BF16), w_conv[l], b_conv[l], w_down[l].astype(BF16))
```

```python
import functools
import math

import numpy as np
import jax
import jax.numpy as jnp
from jax import lax
from jax.experimental import pallas as pl
from jax.experimental.pallas import tpu as pltpu

D_MODEL = 1024
HEAD_DIM = 64
N_HEADS = 8
MIX_WIDTH = N_HEADS * HEAD_DIM
IN_WIDTH = 6 * MIX_WIDTH + 2 * D_MODEL
MOBA_BLOCK = 256
MOBA_TOPK = 3
D_FF = 2816
REL_BUCKETS = 32
REL_MAX_DIST = 128
NORM_EPS = 1e-6
NEG_INF = -1e30
LOG2E = math.log2(math.e)

LANES = 128
PAIR = LANES // HEAD_DIM
N_PAIRS = N_HEADS // PAIR
BLK = MOBA_BLOCK
VMEM_LIMIT = 56 * 1024 * 1024

F32 = jnp.float32
BF16 = jnp.bfloat16


def _dot(a, b):
    return jnp.dot(a, b, preferred_element_type=F32)


def _dot_nt(a, b):
    return lax.dot_general(a, b, (((1,), (1,)), ((), ())), preferred_element_type=F32)


def _rms(x):
    return x * lax.rsqrt(jnp.mean(x * x, axis=-1, keepdims=True) + NORM_EPS)


def _ada_kernel(c_ref, w_ref, b_ref, o_ref):
    c = c_ref[...]
    ca = c * jax.nn.sigmoid(c)
    o_ref[...] = jnp.dot(ca, w_ref[...], precision=lax.Precision.HIGHEST,
                         preferred_element_type=F32) + b_ref[...]


def _ada(c, w_ada, b_ada):
    b, d = c.shape
    n = w_ada.shape[1]
    tn = D_MODEL
    return pl.pallas_call(
        _ada_kernel,
        grid=(n // tn,),
        in_specs=[pl.BlockSpec((b, d), lambda j: (0, 0)),
                  pl.BlockSpec((d, tn), lambda j: (0, j)),
                  pl.BlockSpec((1, tn), lambda j: (0, j))],
        out_specs=pl.BlockSpec((b, tn), lambda j: (0, j)),
        out_shape=jax.ShapeDtypeStruct((b, n), F32),
        compiler_params=pltpu.CompilerParams(dimension_semantics=("arbitrary",),
                                             vmem_limit_bytes=VMEM_LIMIT),
        name="ada",
    )(c, w_ada, b_ada.reshape(1, n))


def _bucket_tiles():
    def bucket(rel):
        n = np.maximum(rel, 0)
        max_exact = REL_BUCKETS // 2
        n_f = np.maximum(n, max_exact).astype(np.float32)
        large = max_exact + (np.log(n_f / np.float32(max_exact)) / np.float32(math.log(REL_MAX_DIST / max_exact))
                             * np.float32(REL_BUCKETS - max_exact)).astype(np.int32)
        large = np.minimum(large, REL_BUCKETS - 1)
        return np.where(n < max_exact, n, large).astype(np.int32)
    qi = np.arange(BLK)[:, None]
    kj = np.arange(BLK)[None, :]
    t0 = np.where(qi >= kj, bucket(qi - kj), REL_BUCKETS)
    t1 = bucket(qi - kj + BLK)
    far = bucket(np.array([2 * BLK - (BLK - 1)]))
    assert int(far[0]) == REL_BUCKETS - 1
    return np.stack([t0, t1]).astype(np.int32)


def _bias_kernel(tab_ref, bkt_ref, o_ref):
    h = pl.program_id(0)
    for t in range(2):
        b = bkt_ref[t]
        acc = jnp.full((BLK, BLK), NEG_INF, F32)
        for k in range(REL_BUCKETS):
            acc = jnp.where(b == k, tab_ref[h, k] * LOG2E, acc)
        o_ref[0, t] = acc
    o_ref[0, 2] = jnp.full((BLK, BLK), tab_ref[h, REL_BUCKETS - 1] * LOG2E, F32)


def _bias_tiles(rel_bias):
    bkt = jnp.asarray(_bucket_tiles())
    return pl.pallas_call(
        _bias_kernel,
        grid=(N_HEADS,),
        in_specs=[pl.BlockSpec(memory_space=pltpu.SMEM),
                  pl.BlockSpec((2, BLK, BLK), lambda h: (0, 0, 0))],
        out_specs=pl.BlockSpec((1, 3, BLK, BLK), lambda h: (h, 0, 0, 0)),
        out_shape=jax.ShapeDtypeStruct((N_HEADS, 3, BLK, BLK), F32),
        compiler_params=pltpu.CompilerParams(dimension_semantics=("arbitrary",)),
        name="bias_tiles",
    )(rel_bias, bkt)


PROJ_TS = 256
PROJ_NC = 512


def _proj_kernel(x_ref, mod_ref, g_ref, w_ref, o_ref):
    x = x_ref[0]
    sh = mod_ref[0, 0:1, :]
    sc = mod_ref[0, 1:2, :]
    h = (_rms(x) * g_ref[...] * (1.0 + sc) + sh).astype(BF16)
    scale = HEAD_DIM ** -0.5
    for n0 in range(0, IN_WIDTH, PROJ_NC):
        r = _dot(h, w_ref[:, n0:n0 + PROJ_NC])
        if n0 in (0, 3 * MIX_WIDTH):
            r = r * (scale * LOG2E)
        elif n0 >= 6 * MIX_WIDTH:
            r = jax.nn.sigmoid(r)
        o_ref[0, :, n0:n0 + PROJ_NC] = r.astype(BF16)


def _proj(x, mod, g_mix, w_in_bf16):
    b, s, d = x.shape
    ts = PROJ_TS
    return pl.pallas_call(
        _proj_kernel,
        grid=(b, s // ts),
        in_specs=[pl.BlockSpec((1, ts, d), lambda i, j: (i, j, 0)),
                  pl.BlockSpec((1, 6, d), lambda i, j: (i, 0, 0)),
                  pl.BlockSpec((1, d), lambda i, j: (0, 0)),
                  pl.BlockSpec((d, IN_WIDTH), lambda i, j: (0, 0), pipeline_mode=pl.Buffered(1))],
        out_specs=pl.BlockSpec((1, ts, IN_WIDTH), lambda i, j: (i, j, 0)),
        out_shape=jax.ShapeDtypeStruct((b, s, IN_WIDTH), BF16),
        compiler_params=pltpu.CompilerParams(dimension_semantics=("arbitrary", "arbitrary"),
                                             vmem_limit_bytes=VMEM_LIMIT),
        name="proj",
    )(x, mod, g_mix.reshape(1, d), w_in_bf16)


ATTN_UNROLL = 16


def _head_lanes(shape, h):
    lane = lax.broadcasted_iota(jnp.int32, shape, len(shape) - 1)
    return (lane >= h * HEAD_DIM) & (lane < (h + 1) * HEAD_DIM)


def _rows(blk):
    return slice(blk * BLK, (blk + 1) * BLK)


def _run_chains(chains, step):
    states = [None] * len(chains)
    for k in range(max(len(c) for c in chains)):
        for ci, chain in enumerate(chains):
            if k < len(chain):
                states[ci] = step(chain[k], states[ci])
    return states


def _pair_chains(t, n_blk):
    qis = (t, n_blk - 1 - t)
    return qis, [[(qi, j, h) for j in range(qi, -1, -1)] for qi in qis for h in range(PAIR)]


def _store_pair(o_ref, qi, out0, out1):
    o_ref[0, _rows(qi), :] = jnp.where(_head_lanes((BLK, LANES), 0), out0, out1).astype(o_ref.dtype)


def _moba_prepare(q_ref, k_ref, qaug_ref, kaug_ref, pen_ref):
    seq = k_ref.shape[1]
    n_blk = seq // BLK
    k = k_ref[0]
    q = q_ref[0]
    sub = 16
    r = lax.broadcasted_iota(jnp.int32, (sub, seq), 0)
    c = lax.broadcasted_iota(jnp.int32, (sub, seq), 1)
    avg = jnp.where(c // BLK == r, 1.0 / BLK, 0.0).astype(BF16)
    kmean = _dot(avg, k)
    row = lax.broadcasted_iota(jnp.int32, (seq, LANES), 0)
    lane = lax.broadcasted_iota(jnp.int32, (seq, LANES), 1)
    rr = lax.broadcasted_iota(jnp.int32, (BLK, BLK), 0)
    cc = lax.broadcasted_iota(jnp.int32, (BLK, BLK), 1)
    eye = jnp.where(rr == cc, 1.0, 0.0).astype(BF16)
    jidx = lax.broadcasted_iota(jnp.int32, (8, seq), 0)
    qblk = lax.broadcasted_iota(jnp.int32, (8, seq), 1) // BLK
    pen_ref[...] = jnp.zeros(pen_ref.shape, F32)
    for h in range(PAIR):
        ind0 = (1 - h) * HEAD_DIM
        ind = (lane - ind0) == (row // BLK)
        kaug_ref[h] = jnp.where(_head_lanes((seq, LANES), h), k, jnp.where(ind, 1.0, 0.0).astype(BF16))
        kmh = jnp.where(_head_lanes((sub, LANES), h), kmean, 0.0)
        hi = kmh.astype(BF16)
        lo = (kmh - hi.astype(F32)).astype(BF16)
        g8 = (_dot_nt(hi, q) + _dot_nt(lo, q))[0:8, :]
        cnt = jnp.zeros((8, seq), jnp.int32)
        for jp in range(n_blk):
            rowv = g8[jp:jp + 1, :]
            beats = (rowv > g8) | ((rowv == g8) & (jp < jidx))
            cnt = cnt + jnp.where(beats & (jp < qblk), 1, 0)
        sel = (jidx == qblk) | ((jidx < qblk) & (cnt < MOBA_TOPK))
        pen_ref[ind0:ind0 + 8, :] = jnp.where(sel, 0.0, NEG_INF)
    for b in range(n_blk):
        pen_cols = _dot_nt(eye, pen_ref[:, _rows(b)].astype(BF16)).astype(BF16)
        for h in range(PAIR):
            qaug_ref[h, _rows(b), :] = jnp.where(_head_lanes((BLK, LANES), h), q[_rows(b), :], pen_cols)


def _moba_pair(t, n_blk, v_ref, bias_ref, o_ref, qaug_ref, kaug_ref):
    def step(args, st):
        qi, j, h = args
        s = _dot_nt(qaug_ref[h, _rows(qi), :], kaug_ref[h, _rows(j), :]) + bias_ref[h, min(qi - j, 2)]
        m_blk = jnp.max(s, axis=-1, keepdims=True)
        vj = v_ref[0, _rows(j), :]
        if st is None:
            p = jnp.exp2(s - m_blk)
            return m_blk, jnp.sum(p, axis=-1, keepdims=True), _dot(p.astype(BF16), vj)
        m, l, acc = st
        m_new = jnp.maximum(m, m_blk)
        alpha = jnp.exp2(m - m_new)
        p = jnp.exp2(s - m_new)
        return (m_new, alpha * l + jnp.sum(p, axis=-1, keepdims=True),
                alpha * acc + _dot(p.astype(BF16), vj))

    qis, chains = _pair_chains(t, n_blk)
    states = _run_chains(chains, step)
    for n, qi in enumerate(qis):
        (_, l0, a0), (_, l1, a1) = states[PAIR * n], states[PAIR * n + 1]
        _store_pair(o_ref, qi, a0 / l0, a1 / l1)


def _moba_kernel(q_ref, k_ref, v_ref, bias_ref, o_ref, qaug_ref, kaug_ref, pen_ref):
    t = pl.program_id(2)
    n_blk = k_ref.shape[1] // BLK

    @pl.when(t == 0)
    def _prepare():
        _moba_prepare(q_ref, k_ref, qaug_ref, kaug_ref, pen_ref)

    for tt in range(n_blk // 2):
        pl.when(t == tt)(functools.partial(_moba_pair, tt, n_blk, v_ref, bias_ref, o_ref, qaug_ref, kaug_ref))


def _moba(proj, bias):
    b, s, _ = proj.shape
    n_blk = s // BLK
    assert s % BLK == 0 and n_blk % 2 == 0 and n_blk <= 8
    full = lambda off: pl.BlockSpec((1, s, LANES), lambda bi, p, t: (bi, 0, off + p))
    return pl.pallas_call(
        _moba_kernel,
        grid=(b, N_PAIRS, n_blk // 2),
        in_specs=[full(0), full(MIX_WIDTH // LANES), full(2 * MIX_WIDTH // LANES),
                  pl.BlockSpec((PAIR, 3, BLK, BLK), lambda bi, p, t: (p, 0, 0, 0))],
        out_specs=pl.BlockSpec((1, s, LANES), lambda bi, p, t: (bi, 0, p)),
        out_shape=jax.ShapeDtypeStruct((b, s, MIX_WIDTH), BF16),
        scratch_shapes=[pltpu.VMEM((PAIR, s, LANES), BF16),
                        pltpu.VMEM((PAIR, s, LANES), BF16),
                        pltpu.VMEM((LANES, s), F32)],
        compiler_params=pltpu.CompilerParams(dimension_semantics=("arbitrary",) * 3,
                                             vmem_limit_bytes=VMEM_LIMIT),
        name="moba",
    )(proj, proj, proj, bias)


SB_MASK = -1e9


def _sb_mask_tiles():
    r = np.arange(BLK)[:, None]
    c = np.arange(BLK)[None, :]
    tri = np.where(c < r, 0.0, SB_MASK)
    return np.stack([np.zeros((BLK, PAIR * BLK)), np.tile(tri, (1, PAIR))]).astype(np.float32)


def _sb_kernel(q_ref, k_ref, v_ref, mask_ref, o_ref, ks_ref, vs_ref, lk_ref, lb_ref, a_ref, acc_ref):
    seq = k_ref.shape[1]
    n_blk = seq // BLK
    n_tiles = n_blk * (n_blk + 1) // 2
    wide = PAIR * BLK
    rr = lax.broadcasted_iota(jnp.int32, (BLK, BLK), 0)
    cc = lax.broadcasted_iota(jnp.int32, (BLK, BLK), 1)
    upper = jnp.where(rr > cc, 1.0, 0.0).astype(BF16)
    sign_bit = jnp.uint32(0x80000000)

    for h in range(PAIR):
        in_head = _head_lanes((BLK, LANES), h)
        for blk in range(n_blk):
            ks_ref[blk, h * BLK:(h + 1) * BLK, :] = jnp.where(in_head, k_ref[0, _rows(blk), :], jnp.zeros((), BF16))
            vs_ref[blk, h * BLK:(h + 1) * BLK, :] = jnp.where(in_head, v_ref[0, _rows(blk), :], jnp.zeros((), BF16))

    def rows(blk):
        return pl.ds(pl.multiple_of(blk * BLK, BLK), BLK)

    def cols(h):
        return slice(h * BLK, (h + 1) * BLK)

    def next_tile(t):
        qi, j = t
        last = j == 0
        return jnp.where(last, qi + 1, qi), jnp.where(last, qi + 1, j - 1)

    def stage1(t, slot, run):
        qi, j = t
        diag = qi == j
        z = _dot_nt(q_ref[0, rows(qi), :], ks_ref[j]) + mask_ref[jnp.where(diag, 1, 0)]
        neg_abs = lax.bitcast_convert_type(lax.bitcast_convert_type(z, jnp.uint32) | sign_bit, F32)
        sp = jnp.log(1.0 + jnp.exp2(neg_abs)) * LOG2E
        log_beta = jnp.minimum(z, 0.0) - sp
        log_keep = log_beta - z
        lk_ref[slot] = log_keep.astype(BF16)
        lb_ref[slot] = log_beta
        keep = jnp.where(diag, 0.0, 1.0)
        through, before = [], []
        for h in range(PAIR):
            prev = run[h] * keep
            before.append(prev)
            through.append(prev + jnp.sum(log_keep[:, cols(h)], axis=-1, keepdims=True))
        return through, before

    def stage2(slot, before):
        for h in range(PAIR):
            later = _dot(lk_ref[slot, :, cols(h)], upper)
            a_ref[slot, :, cols(h)] = jnp.exp2(lb_ref[slot, :, cols(h)] + before[h] + later).astype(BF16)

    def stage3(t, slot):
        qi, j = t
        pv = _dot(a_ref[slot], vs_ref[j])
        acc = jnp.where(qi == j, pv, acc_ref[...] + pv)
        acc_ref[...] = acc
        o_ref[0, rows(qi), :] = acc.astype(o_ref.dtype)

    zero = jnp.zeros((BLK, 1), F32)
    t0 = (jnp.int32(0), jnp.int32(0))
    t1 = next_tile(t0)
    run, before0 = stage1(t0, 0, [zero] * PAIR)
    run, before1 = stage1(t1, 1, run)
    stage2(0, before0)

    def pipeline_step(state, slot):
        t_cur, t_m1, t_m2, run, before_m1 = state
        run, before_cur = stage1(t_cur, slot, run)
        stage2(1 - slot, before_m1)
        stage3(t_m2, slot)
        return next_tile(t_cur), t_cur, t_m1, run, before_cur

    def body(_, state):
        for u in range(ATTN_UNROLL):
            state = pipeline_step(state, u % 2)
        return state

    assert ATTN_UNROLL % 2 == 0 and (n_tiles - 4) % ATTN_UNROLL == 0
    state = lax.fori_loop(0, (n_tiles - 4) // ATTN_UNROLL, body, (next_tile(t1), t1, t0, run, before1))
    state = pipeline_step(state, 0)
    _, t_m1, t_m2, _, before_m1 = pipeline_step(state, 1)
    stage2(1, before_m1)
    stage3(t_m2, 0)
    stage3(t_m1, 1)


def _sb(proj):
    b, s, _ = proj.shape
    assert s % BLK == 0
    base = 3 * MIX_WIDTH // LANES
    full = lambda off: pl.BlockSpec((1, s, LANES), lambda bi, p: (bi, 0, off + p))
    return pl.pallas_call(
        _sb_kernel,
        grid=(b, N_PAIRS),
        in_specs=[full(base), full(base + MIX_WIDTH // LANES), full(base + 2 * MIX_WIDTH // LANES),
                  pl.BlockSpec((2, BLK, PAIR * BLK), lambda bi, p: (0, 0, 0))],
        out_specs=pl.BlockSpec((1, s, LANES), lambda bi, p: (bi, 0, p)),
        out_shape=jax.ShapeDtypeStruct((b, s, MIX_WIDTH), BF16),
        scratch_shapes=[pltpu.VMEM((s // BLK, PAIR * BLK, LANES), BF16),
                        pltpu.VMEM((s // BLK, PAIR * BLK, LANES), BF16),
                        pltpu.VMEM((2, BLK, PAIR * BLK), BF16),
                        pltpu.VMEM((2, BLK, PAIR * BLK), F32),
                        pltpu.VMEM((2, BLK, PAIR * BLK), BF16),
                        pltpu.VMEM((BLK, LANES), F32)],
        compiler_params=pltpu.CompilerParams(dimension_semantics=("arbitrary",) * 2,
                                             vmem_limit_bytes=VMEM_LIMIT),
        name="sb",
    )(proj, proj, proj, jnp.asarray(_sb_mask_tiles()))


POST_TS = 512
MXU_TILE = 256
FF_SPLIT = (D_FF // MXU_TILE + 1) // 2 * MXU_TILE
FF_CHUNKS = ((0, FF_SPLIT), (FF_SPLIT, D_FF - FF_SPLIT))
HALO = 8


def _gelu_tanh(x):
    return 0.5 * x * (1.0 + jnp.tanh(math.sqrt(2.0 / math.pi) * (x + 0.044715 * (x * x * x))))


def _post_kernel(x_ref, oa_ref, ob_ref, ga_ref, gb_ref, mod_ref, gffn_ref, gfin_ref,
                 wbm_ref, wbs_ref, wout_ref, wup_ref, wconv_ref, bconv_ref, wdown_ref,
                 o_ref, halo_ref, ubuf_ref):
    ts = x_ref.shape[1]

    @pl.when(pl.program_id(1) == 0)
    def _zero_halo():
        halo_ref[...] = jnp.zeros_like(halo_ref)

    gt_m = mod_ref[0, 2:3, :]
    sh_f = mod_ref[0, 3:4, :]
    sc_f = mod_ref[0, 4:5, :]
    gt_f = mod_ref[0, 5:6, :]

    ya = _dot(oa_ref[0], wbm_ref[...])
    yb = _dot(ob_ref[0], wbs_ref[...])
    mix = (ga_ref[0].astype(F32) * ya + gb_ref[0].astype(F32) * yb).astype(BF16)
    x1 = x_ref[0] + gt_m * _dot(mix, wout_ref[...])
    h2 = (_rms(x1) * gffn_ref[...] * (1.0 + sc_f) + sh_f).astype(BF16)

    def conv(c0, width, buf):
        cols = slice(c0, c0 + width)
        u = _dot(h2, wup_ref[:, cols])
        ubuf_ref[buf, 0:HALO, 0:width] = halo_ref[:, cols]
        ubuf_ref[buf, HALO:HALO + ts, 0:width] = u
        halo_ref[:, cols] = u[ts - HALO:ts, :]
        p1 = ubuf_ref[buf, HALO - 1:HALO - 1 + ts, 0:width]
        p2 = ubuf_ref[buf, HALO - 2:HALO - 2 + ts, 0:width]
        w = wconv_ref[:, cols]
        return w[0:1, :] * p2 + w[1:2, :] * p1 + w[2:3, :] * u + bconv_ref[:, cols]

    y = jnp.zeros((ts, D_MODEL), F32)
    for c0, width in FF_CHUNKS:
        u_val = conv(c0, width, 0)
        u_gate = conv(D_FF + c0, width, 1)
        act = (_gelu_tanh(u_gate) * u_val).astype(BF16)
        y = y + _dot(act, wdown_ref[c0:c0 + width, :])
    x2 = x1 + gt_f * y
    o_ref[0] = _rms(x2) * gfin_ref[...]


def _post(x, oa, ob, proj, mod, g_ffn, g_final, wbm, wbs, wout, wup, wconv, bconv, wdown):
    b, s, d = x.shape
    ts = POST_TS
    ga_blk = 6 * MIX_WIDTH // D_MODEL
    tok = lambda i, j: (i, j, 0)
    const2 = lambda i, j: (0, 0)
    resident = functools.partial(pl.BlockSpec, index_map=const2, pipeline_mode=pl.Buffered(1))
    return pl.pallas_call(
        _post_kernel,
        grid=(b, s // ts),
        in_specs=[pl.BlockSpec((1, ts, d), tok),
                  pl.BlockSpec((1, ts, MIX_WIDTH), tok),
                  pl.BlockSpec((1, ts, MIX_WIDTH), tok),
                  pl.BlockSpec((1, ts, d), lambda i, j: (i, j, ga_blk)),
                  pl.BlockSpec((1, ts, d), lambda i, j: (i, j, ga_blk + 1)),
                  pl.BlockSpec((1, 6, d), lambda i, j: (i, 0, 0)),
                  pl.BlockSpec((1, d), const2),
                  pl.BlockSpec((1, d), const2),
                  resident((MIX_WIDTH, d)),
                  resident((MIX_WIDTH, d)),
                  resident((d, d)),
                  resident((d, 2 * D_FF)),
                  pl.BlockSpec((3, 2 * D_FF), const2),
                  pl.BlockSpec((1, 2 * D_FF), const2),
                  resident((D_FF, d))],
        out_specs=pl.BlockSpec((1, ts, d), tok),
        out_shape=jax.ShapeDtypeStruct((b, s, d), x.dtype),
        scratch_shapes=[pltpu.VMEM((HALO, 2 * D_FF), F32),
                        pltpu.VMEM((2, HALO + ts, FF_SPLIT), F32)],
        compiler_params=pltpu.CompilerParams(dimension_semantics=("arbitrary", "arbitrary"),
                                             vmem_limit_bytes=VMEM_LIMIT),
        name="post",
    )(x, oa, ob, proj, proj, mod, g_ffn.reshape(1, d), g_final.reshape(1, d),
      wbm, wbs, wout, wup, wconv, bconv.reshape(1, 2 * D_FF), wdown)


@jax.jit
def kernel(x, c, w_ada, b_ada, g_mix, w_in, w_br_moba, w_br_sb, w_out, rel_bias, g_ffn, w_up,
           w_conv, b_conv, w_down, g_final):
    assert w_ada.shape[0] == 1, "the final rms_norm is fused into the single layer's last kernel"
    l = 0
    bias = _bias_tiles(rel_bias)
    mod = _ada(c, w_ada[l], b_ada[l]).reshape(x.shape[0], 6, D_MODEL)
    proj = _proj(x, mod, g_mix[l], w_in[l].astype(BF16))
    oa = _moba(proj, bias)
    ob = _sb(proj)
    return _post(x, oa, ob, proj, mod, g_ffn[l], g_final,
                 w_br_moba[l].astype(BF16), w_br_sb[l].astype(BF16), w_out[l].astype(BF16),
                 w_up[l].astype(BF16), w_conv[l], b_conv[l], w_down[l].astype(BF16))
```

```python
import functools
import math

import numpy as np
import jax
import jax.numpy as jnp
from jax import lax
from jax.experimental import pallas as pl
from jax.experimental.pallas import tpu as pltpu

D_MODEL = 1024
HEAD_DIM = 64
N_HEADS = 8
MIX_WIDTH = N_HEADS * HEAD_DIM
IN_WIDTH = 6 * MIX_WIDTH + 2 * D_MODEL
MOBA_BLOCK = 256
MOBA_TOPK = 3
D_FF = 2816
REL_BUCKETS = 32
REL_MAX_DIST = 128
NORM_EPS = 1e-6
NEG_INF = -1e30
LOG2E = math.log2(math.e)

LANES = 128
PAIR = LANES // HEAD_DIM
N_PAIRS = N_HEADS // PAIR
BLK = MOBA_BLOCK
VMEM_LIMIT = 56 * 1024 * 1024

F32 = jnp.float32
BF16 = jnp.bfloat16


def _dot(a, b):
    return jnp.dot(a, b, preferred_element_type=F32)


def _dot_nt(a, b):
    return lax.dot_general(a, b, (((1,), (1,)), ((), ())), preferred_element_type=F32)


def _rms(x):
    return x * lax.rsqrt(jnp.mean(x * x, axis=-1, keepdims=True) + NORM_EPS)


def _ada_kernel(c_ref, w_ref, b_ref, o_ref):
    c = c_ref[...]
    ca = c * jax.nn.sigmoid(c)
    o_ref[...] = jnp.dot(ca, w_ref[...], precision=lax.Precision.HIGHEST,
                         preferred_element_type=F32) + b_ref[...]


def _ada(c, w_ada, b_ada):
    b, d = c.shape
    n = w_ada.shape[1]
    tn = D_MODEL
    return pl.pallas_call(
        _ada_kernel,
        grid=(n // tn,),
        in_specs=[pl.BlockSpec((b, d), lambda j: (0, 0)),
                  pl.BlockSpec((d, tn), lambda j: (0, j)),
                  pl.BlockSpec((1, tn), lambda j: (0, j))],
        out_specs=pl.BlockSpec((b, tn), lambda j: (0, j)),
        out_shape=jax.ShapeDtypeStruct((b, n), F32),
        compiler_params=pltpu.CompilerParams(dimension_semantics=("arbitrary",),
                                             vmem_limit_bytes=VMEM_LIMIT),
        name="ada",
    )(c, w_ada, b_ada.reshape(1, n))


def _bucket_tiles():
    def bucket(rel):
        n = np.maximum(rel, 0)
        max_exact = REL_BUCKETS // 2
        n_f = np.maximum(n, max_exact).astype(np.float32)
        large = max_exact + (np.log(n_f / np.float32(max_exact)) / np.float32(math.log(REL_MAX_DIST / max_exact))
                             * np.float32(REL_BUCKETS - max_exact)).astype(np.int32)
        large = np.minimum(large, REL_BUCKETS - 1)
        return np.where(n < max_exact, n, large).astype(np.int32)
    qi = np.arange(BLK)[:, None]
    kj = np.arange(BLK)[None, :]
    t0 = np.where(qi >= kj, bucket(qi - kj), REL_BUCKETS)
    t1 = bucket(qi - kj + BLK)
    far = bucket(np.array([2 * BLK - (BLK - 1)]))
    assert int(far[0]) == REL_BUCKETS - 1
    return np.stack([t0, t1]).astype(np.int32)


def _bias_kernel(tab_ref, bkt_ref, o_ref):
    h = pl.program_id(0)
    for t in range(2):
        b = bkt_ref[t]
        acc = jnp.full((BLK, BLK), NEG_INF, F32)
        for k in range(REL_BUCKETS):
            acc = jnp.where(b == k, tab_ref[h, k] * LOG2E, acc)
        o_ref[0, t] = acc
    o_ref[0, 2] = jnp.full((BLK, BLK), tab_ref[h, REL_BUCKETS - 1] * LOG2E, F32)


def _bias_tiles(rel_bias):
    bkt = jnp.asarray(_bucket_tiles())
    return pl.pallas_call(
        _bias_kernel,
        grid=(N_HEADS,),
        in_specs=[pl.BlockSpec(memory_space=pltpu.SMEM),
                  pl.BlockSpec((2, BLK, BLK), lambda h: (0, 0, 0))],
        out_specs=pl.BlockSpec((1, 3, BLK, BLK), lambda h: (h, 0, 0, 0)),
        out_shape=jax.ShapeDtypeStruct((N_HEADS, 3, BLK, BLK), F32),
        compiler_params=pltpu.CompilerParams(dimension_semantics=("arbitrary",)),
        name="bias_tiles",
    )(rel_bias, bkt)


PROJ_TS = 256
PROJ_NC = 512


def _proj_kernel(x_ref, mod_ref, g_ref, w_ref, o_ref):
    x = x_ref[0]
    sh = mod_ref[0, 0:1, :]
    sc = mod_ref[0, 1:2, :]
    h = (_rms(x) * g_ref[...] * (1.0 + sc) + sh).astype(BF16)
    scale = HEAD_DIM ** -0.5
    for n0 in range(0, IN_WIDTH, PROJ_NC):
        r = _dot(h, w_ref[:, n0:n0 + PROJ_NC])
        if n0 in (0, 3 * MIX_WIDTH):
            r = r * (scale * LOG2E)
        elif n0 >= 6 * MIX_WIDTH:
            r = jax.nn.sigmoid(r)
        o_ref[0, :, n0:n0 + PROJ_NC] = r.astype(BF16)


def _proj(x, mod, g_mix, w_in_bf16):
    b, s, d = x.shape
    ts = PROJ_TS
    return pl.pallas_call(
        _proj_kernel,
        grid=(b, s // ts),
        in_specs=[pl.BlockSpec((1, ts, d), lambda i, j: (i, j, 0)),
                  pl.BlockSpec((1, 6, d), lambda i, j: (i, 0, 0)),
                  pl.BlockSpec((1, d), lambda i, j: (0, 0)),
                  pl.BlockSpec((d, IN_WIDTH), lambda i, j: (0, 0), pipeline_mode=pl.Buffered(1))],
        out_specs=pl.BlockSpec((1, ts, IN_WIDTH), lambda i, j: (i, j, 0)),
        out_shape=jax.ShapeDtypeStruct((b, s, IN_WIDTH), BF16),
        compiler_params=pltpu.CompilerParams(dimension_semantics=("arbitrary", "arbitrary"),
                                             vmem_limit_bytes=VMEM_LIMIT),
        name="proj",
    )(x, mod, g_mix.reshape(1, d), w_in_bf16)


ATTN_UNROLL = 16


def _head_lanes(shape, h):
    lane = lax.broadcasted_iota(jnp.int32, shape, len(shape) - 1)
    return (lane >= h * HEAD_DIM) & (lane < (h + 1) * HEAD_DIM)


def _rows(blk):
    return slice(blk * BLK, (blk + 1) * BLK)


def _run_chains(chains, step):
    states = [None] * len(chains)
    for k in range(max(len(c) for c in chains)):
        for ci, chain in enumerate(chains):
            if k < len(chain):
                states[ci] = step(chain[k], states[ci])
    return states


def _pair_chains(t, n_blk):
    qis = (t, n_blk - 1 - t)
    return qis, [[(qi, j, h) for j in range(qi, -1, -1)] for qi in qis for h in range(PAIR)]


def _store_pair(o_ref, qi, out0, out1):
    o_ref[0, _rows(qi), :] = jnp.where(_head_lanes((BLK, LANES), 0), out0, out1).astype(o_ref.dtype)


def _moba_prepare(q_ref, k_ref, qaug_ref, kaug_ref, pen_ref):
    seq = k_ref.shape[1]
    n_blk = seq // BLK
    k = k_ref[0]
    q = q_ref[0]
    sub = 16
    r = lax.broadcasted_iota(jnp.int32, (sub, seq), 0)
    c = lax.broadcasted_iota(jnp.int32, (sub, seq), 1)
    avg = jnp.where(c // BLK == r, 1.0 / BLK, 0.0).astype(BF16)
    kmean = _dot(avg, k)
    row = lax.broadcasted_iota(jnp.int32, (seq, LANES), 0)
    lane = lax.broadcasted_iota(jnp.int32, (seq, LANES), 1)
    rr = lax.broadcasted_iota(jnp.int32, (BLK, BLK), 0)
    cc = lax.broadcasted_iota(jnp.int32, (BLK, BLK), 1)
    eye = jnp.where(rr == cc, 1.0, 0.0).astype(BF16)
    jidx = lax.broadcasted_iota(jnp.int32, (8, seq), 0)
    qblk = lax.broadcasted_iota(jnp.int32, (8, seq), 1) // BLK
    pen_ref[...] = jnp.zeros(pen_ref.shape, F32)
    for h in range(PAIR):
        ind0 = (1 - h) * HEAD_DIM
        ind = (lane - ind0) == (row // BLK)
        kaug_ref[h] = jnp.where(_head_lanes((seq, LANES), h), k, jnp.where(ind, 1.0, 0.0).astype(BF16))
        kmh = jnp.where(_head_lanes((sub, LANES), h), kmean, 0.0)
        hi = kmh.astype(BF16)
        lo = (kmh - hi.astype(F32)).astype(BF16)
        g8 = (_dot_nt(hi, q) + _dot_nt(lo, q))[0:8, :]
        cnt = jnp.zeros((8, seq), jnp.int32)
        for jp in range(n_blk):
            rowv = g8[jp:jp + 1, :]
            beats = (rowv > g8) | ((rowv == g8) & (jp < jidx))
            cnt = cnt + jnp.where(beats & (jp < qblk), 1, 0)
        sel = (jidx == qblk) | ((jidx < qblk) & (cnt < MOBA_TOPK))
        pen_ref[ind0:ind0 + 8, :] = jnp.where(sel, 0.0, NEG_INF)
    for b in range(n_blk):
        pen_cols = _dot_nt(eye, pen_ref[:, _rows(b)].astype(BF16)).astype(BF16)
        for h in range(PAIR):
            qaug_ref[h, _rows(b), :] = jnp.where(_head_lanes((BLK, LANES), h), q[_rows(b), :], pen_cols)


def _moba_pair(t, n_blk, v_ref, bias_ref, o_ref, qaug_ref, kaug_ref):
    def step(args, st):
        qi, j, h = args
        s = _dot_nt(qaug_ref[h, _rows(qi), :], kaug_ref[h, _rows(j), :]) + bias_ref[h, min(qi - j, 2)]
        m_blk = jnp.max(s, axis=-1, keepdims=True)
        vj = v_ref[0, _rows(j), :]
        if st is None:
            p = jnp.exp2(s - m_blk)
            return m_blk, jnp.sum(p, axis=-1, keepdims=True), _dot(p.astype(BF16), vj)
        m, l, acc = st
        m_new = jnp.maximum(m, m_blk)
        alpha = jnp.exp2(m - m_new)
        p = jnp.exp2(s - m_new)
        return (m_new, alpha * l + jnp.sum(p, axis=-1, keepdims=True),
                alpha * acc + _dot(p.astype(BF16), vj))

    qis, chains = _pair_chains(t, n_blk)
    states = _run_chains(chains, step)
    for n, qi in enumerate(qis):
        (_, l0, a0), (_, l1, a1) = states[PAIR * n], states[PAIR * n + 1]
        _store_pair(o_ref, qi, a0 / l0, a1 / l1)


def _moba_kernel(q_ref, k_ref, v_ref, bias_ref, o_ref, qaug_ref, kaug_ref, pen_ref):
    t = pl.program_id(2)
    n_blk = k_ref.shape[1] // BLK

    @pl.when(t == 0)
    def _prepare():
        _moba_prepare(q_ref, k_ref, qaug_ref, kaug_ref, pen_ref)

    for tt in range(n_blk // 2):
        pl.when(t == tt)(functools.partial(_moba_pair, tt, n_blk, v_ref, bias_ref, o_ref, qaug_ref, kaug_ref))


def _moba(proj, bias):
    b, s, _ = proj.shape
    n_blk = s // BLK
    assert s % BLK == 0 and n_blk % 2 == 0 and n_blk <= 8
    full = lambda off: pl.BlockSpec((1, s, LANES), lambda bi, p, t: (bi, 0, off + p))
    return pl.pallas_call(
        _moba_kernel,
        grid=(b, N_PAIRS, n_blk // 2),
        in_specs=[full(0), full(MIX_WIDTH // LANES), full(2 * MIX_WIDTH // LANES),
                  pl.BlockSpec((PAIR, 3, BLK, BLK), lambda bi, p, t: (p, 0, 0, 0))],
        out_specs=pl.BlockSpec((1, s, LANES), lambda bi, p, t: (bi, 0, p)),
        out_shape=jax.ShapeDtypeStruct((b, s, MIX_WIDTH), BF16),
        scratch_shapes=[pltpu.VMEM((PAIR, s, LANES), BF16),
                        pltpu.VMEM((PAIR, s, LANES), BF16),
                        pltpu.VMEM((LANES, s), F32)],
        compiler_params=pltpu.CompilerParams(dimension_semantics=("arbitrary",) * 3,
                                             vmem_limit_bytes=VMEM_LIMIT),
        name="moba",
    )(proj, proj, proj, bias)


SB_MASK = -1e9


def _sb_mask_tiles():
    r = np.arange(BLK)[:, None]
    c = np.arange(BLK)[None, :]
    tri = np.where(c < r, 0.0, SB_MASK)
    return np.stack([np.zeros((BLK, PAIR * BLK)), np.tile(tri, (1, PAIR))]).astype(np.float32)


def _sb_kernel(q_ref, k_ref, v_ref, mask_ref, o_ref, ks_ref, vs_ref, lk_ref, lb_ref, a_ref, acc_ref):
    seq = k_ref.shape[1]
    n_blk = seq // BLK
    n_tiles = n_blk * (n_blk + 1) // 2
    wide = PAIR * BLK
    rr = lax.broadcasted_iota(jnp.int32, (BLK, BLK), 0)
    cc = lax.broadcasted_iota(jnp.int32, (BLK, BLK), 1)
    upper = jnp.where(rr > cc, 1.0, 0.0).astype(BF16)
    sign_bit = jnp.uint32(0x80000000)

    for h in range(PAIR):
        in_head = _head_lanes((BLK, LANES), h)
        for blk in range(n_blk):
            ks_ref[blk, h * BLK:(h + 1) * BLK, :] = jnp.where(in_head, k_ref[0, _rows(blk), :], jnp.zeros((), BF16))
            vs_ref[blk, h * BLK:(h + 1) * BLK, :] = jnp.where(in_head, v_ref[0, _rows(blk), :], jnp.zeros((), BF16))

    def rows(blk):
        return pl.ds(pl.multiple_of(blk * BLK, BLK), BLK)

    def cols(h):
        return slice(h * BLK, (h + 1) * BLK)

    def next_tile(t):
        qi, j = t
        last = j == 0
        return jnp.where(last, qi + 1, qi), jnp.where(last, qi + 1, j - 1)

    def stage1(t, slot, run):
        qi, j = t
        diag = qi == j
        z = _dot_nt(q_ref[0, rows(qi), :], ks_ref[j]) + mask_ref[jnp.where(diag, 1, 0)]
        neg_abs = lax.bitcast_convert_type(lax.bitcast_convert_type(z, jnp.uint32) | sign_bit, F32)
        sp = jnp.log(1.0 + jnp.exp2(neg_abs)) * LOG2E
        log_beta = jnp.minimum(z, 0.0) - sp
        log_keep = log_beta - z
        lk_ref[slot] = log_keep.astype(BF16)
        lb_ref[slot] = log_beta
        keep = jnp.where(diag, 0.0, 1.0)
        through, before = [], []
        for h in range(PAIR):
            prev = run[h] * keep
            before.append(prev)
            through.append(prev + jnp.sum(log_keep[:, cols(h)], axis=-1, keepdims=True))
        return through, before

    def stage2(slot, before):
        for h in range(PAIR):
            later = _dot(lk_ref[slot, :, cols(h)], upper)
            a_ref[slot, :, cols(h)] = jnp.exp2(lb_ref[slot, :, cols(h)] + before[h] + later).astype(BF16)

    def stage3(t, slot):
        qi, j = t
        pv = _dot(a_ref[slot], vs_ref[j])
        acc = jnp.where(qi == j, pv, acc_ref[...] + pv)
        acc_ref[...] = acc
        o_ref[0, rows(qi), :] = acc.astype(o_ref.dtype)

    zero = jnp.zeros((BLK, 1), F32)
    t0 = (jnp.int32(0), jnp.int32(0))
    t1 = next_tile(t0)
    run, before0 = stage1(t0, 0, [zero] * PAIR)
    run, before1 = stage1(t1, 1, run)
    stage2(0, before0)

    def pipeline_step(state, slot):
        t_cur, t_m1, t_m2, run, before_m1 = state
        run, before_cur = stage1(t_cur, slot, run)
        stage2(1 - slot, before_m1)
        stage3(t_m2, slot)
        return next_tile(t_cur), t_cur, t_m1, run, before_cur

    def body(_, state):
        for u in range(ATTN_UNROLL):
            state = pipeline_step(state, u % 2)
        return state

    assert ATTN_UNROLL % 2 == 0 and (n_tiles - 4) % ATTN_UNROLL == 0
    state = lax.fori_loop(0, (n_tiles - 4) // ATTN_UNROLL, body, (next_tile(t1), t1, t0, run, before1))
    state = pipeline_step(state, 0)
    _, t_m1, t_m2, _, before_m1 = pipeline_step(state, 1)
    stage2(1, before_m1)
    stage3(t_m2, 0)
    stage3(t_m1, 1)


def _sb(proj):
    b, s, _ = proj.shape
    assert s % BLK == 0
    base = 3 * MIX_WIDTH // LANES
    full = lambda off: pl.BlockSpec((1, s, LANES), lambda bi, p: (bi, 0, off + p))
    return pl.pallas_call(
        _sb_kernel,
        grid=(b, N_PAIRS),
        in_specs=[full(base), full(base + MIX_WIDTH // LANES), full(base + 2 * MIX_WIDTH // LANES),
                  pl.BlockSpec((2, BLK, PAIR * BLK), lambda bi, p: (0, 0, 0))],
        out_specs=pl.BlockSpec((1, s, LANES), lambda bi, p: (bi, 0, p)),
        out_shape=jax.ShapeDtypeStruct((b, s, MIX_WIDTH), BF16),
        scratch_shapes=[pltpu.VMEM((s // BLK, PAIR * BLK, LANES), BF16),
                        pltpu.VMEM((s // BLK, PAIR * BLK, LANES), BF16),
                        pltpu.VMEM((2, BLK, PAIR * BLK), BF16),
                        pltpu.VMEM((2, BLK, PAIR * BLK), F32),
                        pltpu.VMEM((2, BLK, PAIR * BLK), BF16),
                        pltpu.VMEM((BLK, LANES), F32)],
        compiler_params=pltpu.CompilerParams(dimension_semantics=("arbitrary",) * 2,
                                             vmem_limit_bytes=VMEM_LIMIT),
        name="sb",
    )(proj, proj, proj, jnp.asarray(_sb_mask_tiles()))


POST_TS = 512
POST_SPLIT = 2
MXU_TILE = 256
FF_SPLIT = (D_FF // MXU_TILE + 1) // 2 * MXU_TILE
FF_CHUNKS = ((0, FF_SPLIT), (FF_SPLIT, D_FF - FF_SPLIT))
HALO = 8


def _gelu_tanh(x):
    return 0.5 * x * (1.0 + jnp.tanh(math.sqrt(2.0 / math.pi) * (x + 0.044715 * (x * x * x))))


def _post_kernel(x_ref, oa_ref, ob_ref, ga_ref, gb_ref, mod_ref, gffn_ref, gfin_ref,
                 wbm_ref, wbs_ref, wout_ref, wup_ref, wconv_ref, bconv_ref, wdown_ref,
                 o_ref, halo_ref, ubuf_ref):
    ts = x_ref.shape[1]

    @pl.when(pl.program_id(1) == 0)
    def _zero_halo():
        halo_ref[...] = jnp.zeros_like(halo_ref)

    gt_m = mod_ref[0, 2:3, :]
    sh_f = mod_ref[0, 3:4, :]
    sc_f = mod_ref[0, 4:5, :]
    gt_f = mod_ref[0, 5:6, :]

    halves = [slice(i * ts // POST_SPLIT, (i + 1) * ts // POST_SPLIT) for i in range(POST_SPLIT)]
    x1, h2 = [], []
    for rows in halves:
        ya = _dot(oa_ref[0, rows, :], wbm_ref[...])
        yb = _dot(ob_ref[0, rows, :], wbs_ref[...])
        mix = (ga_ref[0, rows, :].astype(F32) * ya + gb_ref[0, rows, :].astype(F32) * yb).astype(BF16)
        x1.append(x_ref[0, rows, :] + gt_m * _dot(mix, wout_ref[...]))
        h2.append((_rms(x1[-1]) * gffn_ref[...] * (1.0 + sc_f) + sh_f).astype(BF16))

    def conv(c0, width, buf):
        cols = slice(c0, c0 + width)
        ubuf_ref[buf, 0:HALO, 0:width] = halo_ref[:, cols]
        for rows, h2_rows in zip(halves, h2):
            ubuf_ref[buf, HALO + rows.start:HALO + rows.stop, 0:width] = _dot(h2_rows, wup_ref[:, cols])
        u = ubuf_ref[buf, HALO:HALO + ts, 0:width]
        halo_ref[:, cols] = ubuf_ref[buf, ts:HALO + ts, 0:width]
        p1 = ubuf_ref[buf, HALO - 1:HALO - 1 + ts, 0:width]
        p2 = ubuf_ref[buf, HALO - 2:HALO - 2 + ts, 0:width]
        w = wconv_ref[:, cols]
        return w[0:1, :] * p2 + w[1:2, :] * p1 + w[2:3, :] * u + bconv_ref[:, cols]

    y = [jnp.zeros((ts // POST_SPLIT, D_MODEL), F32)] * POST_SPLIT
    for c0, width in FF_CHUNKS:
        u_val = conv(c0, width, 0)
        u_gate = conv(D_FF + c0, width, 1)
        act = (_gelu_tanh(u_gate) * u_val).astype(BF16)
        y = [y_rows + _dot(act[rows, :], wdown_ref[c0:c0 + width, :]) for rows, y_rows in zip(halves, y)]
    for rows, x1_rows, y_rows in zip(halves, x1, y):
        o_ref[0, rows, :] = _rms(x1_rows + gt_f * y_rows) * gfin_ref[...]


def _post(x, oa, ob, proj, mod, g_ffn, g_final, wbm, wbs, wout, wup, wconv, bconv, wdown):
    b, s, d = x.shape
    ts = POST_TS
    ga_blk = 6 * MIX_WIDTH // D_MODEL
    tok = lambda i, j: (i, j, 0)
    const2 = lambda i, j: (0, 0)
    resident = functools.partial(pl.BlockSpec, index_map=const2, pipeline_mode=pl.Buffered(1))
    return pl.pallas_call(
        _post_kernel,
        grid=(b, s // ts),
        in_specs=[pl.BlockSpec((1, ts, d), tok),
                  pl.BlockSpec((1, ts, MIX_WIDTH), tok),
                  pl.BlockSpec((1, ts, MIX_WIDTH), tok),
                  pl.BlockSpec((1, ts, d), lambda i, j: (i, j, ga_blk)),
                  pl.BlockSpec((1, ts, d), lambda i, j: (i, j, ga_blk + 1)),
                  pl.BlockSpec((1, 6, d), lambda i, j: (i, 0, 0)),
                  pl.BlockSpec((1, d), const2),
                  pl.BlockSpec((1, d), const2),
                  resident((MIX_WIDTH, d)),
                  resident((MIX_WIDTH, d)),
                  resident((d, d)),
                  resident((d, 2 * D_FF)),
                  pl.BlockSpec((3, 2 * D_FF), const2),
                  pl.BlockSpec((1, 2 * D_FF), const2),
                  resident((D_FF, d))],
        out_specs=pl.BlockSpec((1, ts, d), tok),
        out_shape=jax.ShapeDtypeStruct((b, s, d), x.dtype),
        scratch_shapes=[pltpu.VMEM((HALO, 2 * D_FF), F32),
                        pltpu.VMEM((2, HALO + ts, FF_SPLIT), F32)],
        compiler_params=pltpu.CompilerParams(dimension_semantics=("arbitrary", "arbitrary"),
                                             vmem_limit_bytes=VMEM_LIMIT),
        name="post",
    )(x, oa, ob, proj, proj, mod, g_ffn.reshape(1, d), g_final.reshape(1, d),
      wbm, wbs, wout, wup, wconv, bconv.reshape(1, 2 * D_FF), wdown)


@jax.jit
def kernel(x, c, w_ada, b_ada, g_mix, w_in, w_br_moba, w_br_sb, w_out, rel_bias, g_ffn, w_up,
           w_conv, b_conv, w_down, g_final):
    assert w_ada.shape[0] == 1, "the final rms_norm is fused into the single layer's last kernel"
    l = 0
    bias = _bias_tiles(rel_bias)
    mod = _ada(c, w_ada[l], b_ada[l]).reshape(x.shape[0], 6, D_MODEL)
    proj = _proj(x, mod, g_mix[l], w_in[l].astype(BF16))
    oa = _moba(proj, bias)
    ob = _sb(proj)
    return _post(x, oa, ob, proj, mod, g_ffn[l], g_final,
                 w_br_moba[l].astype(BF16), w_br_sb[l].astype(BF16), w_out[l].astype(BF16),
                 w_up[l].astype(BF16), w_conv[l], b_conv[l], w_down[l].astype(BF16))
```

```python
import functools
import math

import numpy as np
import jax
import jax.numpy as jnp
from jax import lax
from jax.experimental import pallas as pl
from jax.experimental.pallas import tpu as pltpu

D_MODEL = 1024
HEAD_DIM = 64
N_HEADS = 8
MIX_WIDTH = N_HEADS * HEAD_DIM
IN_WIDTH = 6 * MIX_WIDTH + 2 * D_MODEL
MOBA_BLOCK = 256
MOBA_TOPK = 3
D_FF = 2816
REL_BUCKETS = 32
REL_MAX_DIST = 128
NORM_EPS = 1e-6
NEG_INF = -1e30
LOG2E = math.log2(math.e)

LANES = 128
PAIR = LANES // HEAD_DIM
N_PAIRS = N_HEADS // PAIR
BLK = MOBA_BLOCK
VMEM_LIMIT = 56 * 1024 * 1024

F32 = jnp.float32
BF16 = jnp.bfloat16


def _dot(a, b):
    return jnp.dot(a, b, preferred_element_type=F32)


def _dot_nt(a, b):
    return lax.dot_general(a, b, (((1,), (1,)), ((), ())), preferred_element_type=F32)


def _rms(x):
    return x * lax.rsqrt(jnp.mean(x * x, axis=-1, keepdims=True) + NORM_EPS)


def _ada_kernel(c_ref, w_ref, b_ref, o_ref):
    c = c_ref[...]
    ca = c * jax.nn.sigmoid(c)
    o_ref[...] = jnp.dot(ca, w_ref[...], precision=lax.Precision.HIGHEST,
                         preferred_element_type=F32) + b_ref[...]


def _ada(c, w_ada, b_ada):
    b, d = c.shape
    n = w_ada.shape[1]
    tn = D_MODEL
    return pl.pallas_call(
        _ada_kernel,
        grid=(n // tn,),
        in_specs=[pl.BlockSpec((b, d), lambda j: (0, 0)),
                  pl.BlockSpec((d, tn), lambda j: (0, j)),
                  pl.BlockSpec((1, tn), lambda j: (0, j))],
        out_specs=pl.BlockSpec((b, tn), lambda j: (0, j)),
        out_shape=jax.ShapeDtypeStruct((b, n), F32),
        compiler_params=pltpu.CompilerParams(dimension_semantics=("arbitrary",),
                                             vmem_limit_bytes=VMEM_LIMIT),
        name="ada",
    )(c, w_ada, b_ada.reshape(1, n))


def _bucket_tiles():
    def bucket(rel):
        n = np.maximum(rel, 0)
        max_exact = REL_BUCKETS // 2
        n_f = np.maximum(n, max_exact).astype(np.float32)
        large = max_exact + (np.log(n_f / np.float32(max_exact)) / np.float32(math.log(REL_MAX_DIST / max_exact))
                             * np.float32(REL_BUCKETS - max_exact)).astype(np.int32)
        large = np.minimum(large, REL_BUCKETS - 1)
        return np.where(n < max_exact, n, large).astype(np.int32)
    qi = np.arange(BLK)[:, None]
    kj = np.arange(BLK)[None, :]
    t0 = np.where(qi >= kj, bucket(qi - kj), REL_BUCKETS)
    t1 = bucket(qi - kj + BLK)
    far = bucket(np.array([2 * BLK - (BLK - 1)]))
    assert int(far[0]) == REL_BUCKETS - 1
    return np.stack([t0, t1]).astype(np.int32)


def _bias_kernel(tab_ref, bkt_ref, o_ref):
    h = pl.program_id(0)
    for t in range(2):
        b = bkt_ref[t]
        acc = jnp.full((BLK, BLK), NEG_INF, F32)
        for k in range(REL_BUCKETS):
            acc = jnp.where(b == k, tab_ref[h, k] * LOG2E, acc)
        o_ref[0, t] = acc
    o_ref[0, 2] = jnp.full((BLK, BLK), tab_ref[h, REL_BUCKETS - 1] * LOG2E, F32)


def _bias_tiles(rel_bias):
    bkt = jnp.asarray(_bucket_tiles())
    return pl.pallas_call(
        _bias_kernel,
        grid=(N_HEADS,),
        in_specs=[pl.BlockSpec(memory_space=pltpu.SMEM),
                  pl.BlockSpec((2, BLK, BLK), lambda h: (0, 0, 0))],
        out_specs=pl.BlockSpec((1, 3, BLK, BLK), lambda h: (h, 0, 0, 0)),
        out_shape=jax.ShapeDtypeStruct((N_HEADS, 3, BLK, BLK), F32),
        compiler_params=pltpu.CompilerParams(dimension_semantics=("arbitrary",)),
        name="bias_tiles",
    )(rel_bias, bkt)


PROJ_TS = 512
PROJ_SPLIT = 2
PROJ_NC = 512


def _proj_kernel(x_ref, mod_ref, g_ref, w_ref, o_ref):
    ts = x_ref.shape[1]
    sh = mod_ref[0, 0:1, :]
    sc = mod_ref[0, 1:2, :]
    groups = [slice(i * ts // PROJ_SPLIT, (i + 1) * ts // PROJ_SPLIT) for i in range(PROJ_SPLIT)]
    hs = [(_rms(x_ref[0, rows, :]) * g_ref[...] * (1.0 + sc) + sh).astype(BF16) for rows in groups]
    scale = HEAD_DIM ** -0.5
    for n0 in range(0, IN_WIDTH, PROJ_NC):
        for rows, h in zip(groups, hs):
            r = _dot(h, w_ref[:, n0:n0 + PROJ_NC])
            if n0 in (0, 3 * MIX_WIDTH):
                r = r * (scale * LOG2E)
            elif n0 >= 6 * MIX_WIDTH:
                r = jax.nn.sigmoid(r)
            o_ref[0, rows, n0:n0 + PROJ_NC] = r.astype(BF16)


def _proj(x, mod, g_mix, w_in_bf16):
    b, s, d = x.shape
    ts = PROJ_TS
    return pl.pallas_call(
        _proj_kernel,
        grid=(b, s // ts),
        in_specs=[pl.BlockSpec((1, ts, d), lambda i, j: (i, j, 0)),
                  pl.BlockSpec((1, 6, d), lambda i, j: (i, 0, 0)),
                  pl.BlockSpec((1, d), lambda i, j: (0, 0)),
                  pl.BlockSpec((d, IN_WIDTH), lambda i, j: (0, 0), pipeline_mode=pl.Buffered(1))],
        out_specs=pl.BlockSpec((1, ts, IN_WIDTH), lambda i, j: (i, j, 0)),
        out_shape=jax.ShapeDtypeStruct((b, s, IN_WIDTH), BF16),
        compiler_params=pltpu.CompilerParams(dimension_semantics=("arbitrary", "arbitrary"),
                                             vmem_limit_bytes=VMEM_LIMIT),
        name="proj",
    )(x, mod, g_mix.reshape(1, d), w_in_bf16)


ATTN_UNROLL = 16


def _head_lanes(shape, h):
    lane = lax.broadcasted_iota(jnp.int32, shape, len(shape) - 1)
    return (lane >= h * HEAD_DIM) & (lane < (h + 1) * HEAD_DIM)


def _rows(blk):
    return slice(blk * BLK, (blk + 1) * BLK)


def _run_chains(chains, step):
    states = [None] * len(chains)
    for k in range(max(len(c) for c in chains)):
        for ci, chain in enumerate(chains):
            if k < len(chain):
                states[ci] = step(chain[k], states[ci])
    return states


def _pair_chains(t, n_blk):
    qis = (t, n_blk - 1 - t)
    return qis, [[(qi, j, h) for j in range(qi, -1, -1)] for qi in qis for h in range(PAIR)]


def _store_pair(o_ref, qi, out0, out1):
    o_ref[0, _rows(qi), :] = jnp.where(_head_lanes((BLK, LANES), 0), out0, out1).astype(o_ref.dtype)


def _moba_prepare(q_ref, k_ref, qaug_ref, kaug_ref, pen_ref):
    seq = k_ref.shape[1]
    n_blk = seq // BLK
    k = k_ref[0]
    q = q_ref[0]
    sub = 16
    r = lax.broadcasted_iota(jnp.int32, (sub, seq), 0)
    c = lax.broadcasted_iota(jnp.int32, (sub, seq), 1)
    avg = jnp.where(c // BLK == r, 1.0 / BLK, 0.0).astype(BF16)
    kmean = _dot(avg, k)
    row = lax.broadcasted_iota(jnp.int32, (seq, LANES), 0)
    lane = lax.broadcasted_iota(jnp.int32, (seq, LANES), 1)
    rr = lax.broadcasted_iota(jnp.int32, (BLK, BLK), 0)
    cc = lax.broadcasted_iota(jnp.int32, (BLK, BLK), 1)
    eye = jnp.where(rr == cc, 1.0, 0.0).astype(BF16)
    jidx = lax.broadcasted_iota(jnp.int32, (8, seq), 0)
    qblk = lax.broadcasted_iota(jnp.int32, (8, seq), 1) // BLK
    pen_ref[...] = jnp.zeros(pen_ref.shape, F32)
    for h in range(PAIR):
        ind0 = (1 - h) * HEAD_DIM
        ind = (lane - ind0) == (row // BLK)
        kaug_ref[h] = jnp.where(_head_lanes((seq, LANES), h), k, jnp.where(ind, 1.0, 0.0).astype(BF16))
        kmh = jnp.where(_head_lanes((sub, LANES), h), kmean, 0.0)
        hi = kmh.astype(BF16)
        lo = (kmh - hi.astype(F32)).astype(BF16)
        g8 = (_dot_nt(hi, q) + _dot_nt(lo, q))[0:8, :]
        cnt = jnp.zeros((8, seq), jnp.int32)
        for jp in range(n_blk):
            rowv = g8[jp:jp + 1, :]
            beats = (rowv > g8) | ((rowv == g8) & (jp < jidx))
            cnt = cnt + jnp.where(beats & (jp < qblk), 1, 0)
        sel = (jidx == qblk) | ((jidx < qblk) & (cnt < MOBA_TOPK))
        pen_ref[ind0:ind0 + 8, :] = jnp.where(sel, 0.0, NEG_INF)
    for b in range(n_blk):
        pen_cols = _dot_nt(eye, pen_ref[:, _rows(b)].astype(BF16)).astype(BF16)
        for h in range(PAIR):
            qaug_ref[h, _rows(b), :] = jnp.where(_head_lanes((BLK, LANES), h), q[_rows(b), :], pen_cols)


def _moba_pair(t, n_blk, v_ref, bias_ref, o_ref, qaug_ref, kaug_ref):
    def step(args, st):
        qi, j, h = args
        s = _dot_nt(qaug_ref[h, _rows(qi), :], kaug_ref[h, _rows(j), :]) + bias_ref[h, min(qi - j, 2)]
        m_blk = jnp.max(s, axis=-1, keepdims=True)
        vj = v_ref[0, _rows(j), :]
        if st is None:
            p = jnp.exp2(s - m_blk)
            return m_blk, jnp.sum(p, axis=-1, keepdims=True), _dot(p.astype(BF16), vj)
        m, l, acc = st
        m_new = jnp.maximum(m, m_blk)
        alpha = jnp.exp2(m - m_new)
        p = jnp.exp2(s - m_new)
        return (m_new, alpha * l + jnp.sum(p, axis=-1, keepdims=True),
                alpha * acc + _dot(p.astype(BF16), vj))

    qis, chains = _pair_chains(t, n_blk)
    states = _run_chains(chains, step)
    for n, qi in enumerate(qis):
        (_, l0, a0), (_, l1, a1) = states[PAIR * n], states[PAIR * n + 1]
        _store_pair(o_ref, qi, a0 / l0, a1 / l1)


def _moba_kernel(q_ref, k_ref, v_ref, bias_ref, o_ref, qaug_ref, kaug_ref, pen_ref):
    t = pl.program_id(2)
    n_blk = k_ref.shape[1] // BLK

    @pl.when(t == 0)
    def _prepare():
        _moba_prepare(q_ref, k_ref, qaug_ref, kaug_ref, pen_ref)

    for tt in range(n_blk // 2):
        pl.when(t == tt)(functools.partial(_moba_pair, tt, n_blk, v_ref, bias_ref, o_ref, qaug_ref, kaug_ref))


def _moba(proj, bias):
    b, s, _ = proj.shape
    n_blk = s // BLK
    assert s % BLK == 0 and n_blk % 2 == 0 and n_blk <= 8
    full = lambda off: pl.BlockSpec((1, s, LANES), lambda bi, p, t: (bi, 0, off + p))
    return pl.pallas_call(
        _moba_kernel,
        grid=(b, N_PAIRS, n_blk // 2),
        in_specs=[full(0), full(MIX_WIDTH // LANES), full(2 * MIX_WIDTH // LANES),
                  pl.BlockSpec((PAIR, 3, BLK, BLK), lambda bi, p, t: (p, 0, 0, 0))],
        out_specs=pl.BlockSpec((1, s, LANES), lambda bi, p, t: (bi, 0, p)),
        out_shape=jax.ShapeDtypeStruct((b, s, MIX_WIDTH), BF16),
        scratch_shapes=[pltpu.VMEM((PAIR, s, LANES), BF16),
                        pltpu.VMEM((PAIR, s, LANES), BF16),
                        pltpu.VMEM((LANES, s), F32)],
        compiler_params=pltpu.CompilerParams(dimension_semantics=("arbitrary",) * 3,
                                             vmem_limit_bytes=VMEM_LIMIT),
        name="moba",
    )(proj, proj, proj, bias)


SB_MASK = -1e9


def _sb_mask_tiles():
    r = np.arange(BLK)[:, None]
    c = np.arange(BLK)[None, :]
    tri = np.where(c < r, 0.0, SB_MASK)
    return np.stack([np.zeros((BLK, PAIR * BLK)), np.tile(tri, (1, PAIR))]).astype(np.float32)


def _sb_kernel(q_ref, k_ref, v_ref, mask_ref, o_ref, ks_ref, vs_ref, lk_ref, lb_ref, a_ref, acc_ref):
    seq = k_ref.shape[1]
    n_blk = seq // BLK
    n_tiles = n_blk * (n_blk + 1) // 2
    rr = lax.broadcasted_iota(jnp.int32, (BLK, BLK), 0)
    cc = lax.broadcasted_iota(jnp.int32, (BLK, BLK), 1)
    upper = jnp.where(rr > cc, 1.0, 0.0).astype(BF16)
    sign_bit = jnp.uint32(0x80000000)

    for h in range(PAIR):
        in_head = _head_lanes((BLK, LANES), h)
        for blk in range(n_blk):
            ks_ref[blk, h * BLK:(h + 1) * BLK, :] = jnp.where(in_head, k_ref[0, _rows(blk), :], jnp.zeros((), BF16))
            vs_ref[blk, h * BLK:(h + 1) * BLK, :] = jnp.where(in_head, v_ref[0, _rows(blk), :], jnp.zeros((), BF16))

    def rows(blk):
        return pl.ds(pl.multiple_of(blk * BLK, BLK), BLK)

    def cols(h):
        return slice(h * BLK, (h + 1) * BLK)

    def next_tile(t):
        qi, j = t
        last = j == 0
        return jnp.where(last, qi + 1, qi), jnp.where(last, qi + 1, j - 1)

    def stage1(t, slot, run):
        qi, j = t
        diag = qi == j
        z = _dot_nt(q_ref[0, rows(qi), :], ks_ref[j]) + mask_ref[jnp.where(diag, 1, 0)]
        neg_abs = lax.bitcast_convert_type(lax.bitcast_convert_type(z, jnp.uint32) | sign_bit, F32)
        sp = jnp.log(1.0 + jnp.exp2(neg_abs)) * LOG2E
        log_beta = jnp.minimum(z, 0.0) - sp
        log_keep = log_beta - z
        lk_ref[slot] = log_keep.astype(BF16)
        lb_ref[slot] = log_beta
        keep = jnp.where(diag, 0.0, 1.0)
        through, before = [], []
        for h in range(PAIR):
            prev = run[h] * keep
            before.append(prev)
            through.append(prev + jnp.sum(log_keep[:, cols(h)], axis=-1, keepdims=True))
        return through, before

    def stage2(slot, before):
        for h in range(PAIR):
            later = _dot(lk_ref[slot, :, cols(h)], upper)
            a_ref[slot, :, cols(h)] = jnp.exp2(lb_ref[slot, :, cols(h)] + before[h] + later).astype(BF16)

    def stage3(t, slot):
        qi, j = t
        pv = _dot(a_ref[slot], vs_ref[j])
        acc = jnp.where(qi == j, pv, acc_ref[...] + pv)
        acc_ref[...] = acc
        o_ref[0, rows(qi), :] = acc.astype(o_ref.dtype)

    zero = jnp.zeros((BLK, 1), F32)
    t0 = (jnp.int32(0), jnp.int32(0))
    t1 = next_tile(t0)
    run, before0 = stage1(t0, 0, [zero] * PAIR)
    run, before1 = stage1(t1, 1, run)
    stage2(0, before0)

    def pipeline_step(state, slot):
        t_cur, t_m1, t_m2, run, before_m1 = state
        run, before_cur = stage1(t_cur, slot, run)
        stage2(1 - slot, before_m1)
        stage3(t_m2, slot)
        return next_tile(t_cur), t_cur, t_m1, run, before_cur

    def body(_, state):
        for u in range(ATTN_UNROLL):
            state = pipeline_step(state, u % 2)
        return state

    assert ATTN_UNROLL % 2 == 0 and (n_tiles - 4) % ATTN_UNROLL == 0
    state = lax.fori_loop(0, (n_tiles - 4) // ATTN_UNROLL, body, (next_tile(t1), t1, t0, run, before1))
    state = pipeline_step(state, 0)
    _, t_m1, t_m2, _, before_m1 = pipeline_step(state, 1)
    stage2(1, before_m1)
    stage3(t_m2, 0)
    stage3(t_m1, 1)


def _sb(proj):
    b, s, _ = proj.shape
    assert s % BLK == 0
    base = 3 * MIX_WIDTH // LANES
    full = lambda off: pl.BlockSpec((1, s, LANES), lambda bi, p: (bi, 0, off + p))
    return pl.pallas_call(
        _sb_kernel,
        grid=(b, N_PAIRS),
        in_specs=[full(base), full(base + MIX_WIDTH // LANES), full(base + 2 * MIX_WIDTH // LANES),
                  pl.BlockSpec((2, BLK, PAIR * BLK), lambda bi, p: (0, 0, 0))],
        out_specs=pl.BlockSpec((1, s, LANES), lambda bi, p: (bi, 0, p)),
        out_shape=jax.ShapeDtypeStruct((b, s, MIX_WIDTH), BF16),
        scratch_shapes=[pltpu.VMEM((s // BLK, PAIR * BLK, LANES), BF16),
                        pltpu.VMEM((s // BLK, PAIR * BLK, LANES), BF16),
                        pltpu.VMEM((2, BLK, PAIR * BLK), BF16),
                        pltpu.VMEM((2, BLK, PAIR * BLK), F32),
                        pltpu.VMEM((2, BLK, PAIR * BLK), BF16),
                        pltpu.VMEM((BLK, LANES), F32)],
        compiler_params=pltpu.CompilerParams(dimension_semantics=("arbitrary",) * 2,
                                             vmem_limit_bytes=VMEM_LIMIT),
        name="sb",
    )(proj, proj, proj, jnp.asarray(_sb_mask_tiles()))


POST_TS = 512
POST_SPLIT = 2
MXU_TILE = 256
FF_SPLIT = (D_FF // MXU_TILE + 1) // 2 * MXU_TILE
FF_CHUNKS = ((0, FF_SPLIT), (FF_SPLIT, D_FF - FF_SPLIT))
HALO = 8


def _gelu_tanh(x):
    return 0.5 * x * (1.0 + jnp.tanh(math.sqrt(2.0 / math.pi) * (x + 0.044715 * (x * x * x))))


def _post_kernel(x_ref, oa_ref, ob_ref, ga_ref, gb_ref, mod_ref, gffn_ref, gfin_ref,
                 wbm_ref, wbs_ref, wout_ref, wup_ref, wconv_ref, bconv_ref, wdown_ref,
                 o_ref, halo_ref, ubuf_ref):
    ts = x_ref.shape[1]

    @pl.when(pl.program_id(1) == 0)
    def _zero_halo():
        halo_ref[...] = jnp.zeros_like(halo_ref)

    gt_m = mod_ref[0, 2:3, :]
    sh_f = mod_ref[0, 3:4, :]
    sc_f = mod_ref[0, 4:5, :]
    gt_f = mod_ref[0, 5:6, :]

    halves = [slice(i * ts // POST_SPLIT, (i + 1) * ts // POST_SPLIT) for i in range(POST_SPLIT)]
    x1, h2 = [], []
    for rows in halves:
        ya = _dot(oa_ref[0, rows, :], wbm_ref[...])
        yb = _dot(ob_ref[0, rows, :], wbs_ref[...])
        mix = (ga_ref[0, rows, :].astype(F32) * ya + gb_ref[0, rows, :].astype(F32) * yb).astype(BF16)
        x1.append(x_ref[0, rows, :] + gt_m * _dot(mix, wout_ref[...]))
        h2.append((_rms(x1[-1]) * gffn_ref[...] * (1.0 + sc_f) + sh_f).astype(BF16))

    def conv(c0, width, buf):
        cols = slice(c0, c0 + width)
        ubuf_ref[buf, 0:HALO, 0:width] = halo_ref[:, cols]
        for rows, h2_rows in zip(halves, h2):
            ubuf_ref[buf, HALO + rows.start:HALO + rows.stop, 0:width] = _dot(h2_rows, wup_ref[:, cols])
        u = ubuf_ref[buf, HALO:HALO + ts, 0:width]
        halo_ref[:, cols] = ubuf_ref[buf, ts:HALO + ts, 0:width]
        p1 = ubuf_ref[buf, HALO - 1:HALO - 1 + ts, 0:width]
        p2 = ubuf_ref[buf, HALO - 2:HALO - 2 + ts, 0:width]
        w = wconv_ref[:, cols]
        return w[0:1, :] * p2 + w[1:2, :] * p1 + w[2:3, :] * u + bconv_ref[:, cols]

    y = [jnp.zeros((ts // POST_SPLIT, D_MODEL), F32)] * POST_SPLIT
    for c0, width in FF_CHUNKS:
        u_val = conv(c0, width, 0)
        u_gate = conv(D_FF + c0, width, 1)
        act = (_gelu_tanh(u_gate) * u_val).astype(BF16)
        y = [y_rows + _dot(act[rows, :], wdown_ref[c0:c0 + width, :]) for rows, y_rows in zip(halves, y)]
    for rows, x1_rows, y_rows in zip(halves, x1, y):
        o_ref[0, rows, :] = _rms(x1_rows + gt_f * y_rows) * gfin_ref[...]


def _post(x, oa, ob, proj, mod, g_ffn, g_final, wbm, wbs, wout, wup, wconv, bconv, wdown):
    b, s, d = x.shape
    ts = POST_TS
    ga_blk = 6 * MIX_WIDTH // D_MODEL
    tok = lambda i, j: (i, j, 0)
    const2 = lambda i, j: (0, 0)
    resident = functools.partial(pl.BlockSpec, index_map=const2, pipeline_mode=pl.Buffered(1))
    return pl.pallas_call(
        _post_kernel,
        grid=(b, s // ts),
        in_specs=[pl.BlockSpec((1, ts, d), tok),
                  pl.BlockSpec((1, ts, MIX_WIDTH), tok),
                  pl.BlockSpec((1, ts, MIX_WIDTH), tok),
                  pl.BlockSpec((1, ts, d), lambda i, j: (i, j, ga_blk)),
                  pl.BlockSpec((1, ts, d), lambda i, j: (i, j, ga_blk + 1)),
                  pl.BlockSpec((1, 6, d), lambda i, j: (i, 0, 0)),
                  pl.BlockSpec((1, d), const2),
                  pl.BlockSpec((1, d), const2),
                  resident((MIX_WIDTH, d)),
                  resident((MIX_WIDTH, d)),
                  resident((d, d)),
                  resident((d, 2 * D_FF)),
                  pl.BlockSpec((3, 2 * D_FF), const2),
                  pl.BlockSpec((1, 2 * D_FF), const2),
                  resident((D_FF, d))],
        out_specs=pl.BlockSpec((1, ts, d), tok),
        out_shape=jax.ShapeDtypeStruct((b, s, d), x.dtype),
        scratch_shapes=[pltpu.VMEM((HALO, 2 * D_FF), F32),
                        pltpu.VMEM((2, HALO + ts, FF_SPLIT), F32)],
        compiler_params=pltpu.CompilerParams(dimension_semantics=("arbitrary", "arbitrary"),
                                             vmem_limit_bytes=VMEM_LIMIT),
        name="post",
    )(x, oa, ob, proj, proj, mod, g_ffn.reshape(1, d), g_final.reshape(1, d),
      wbm, wbs, wout, wup, wconv, bconv.reshape(1, 2 * D_FF), wdown)


@jax.jit
def kernel(x, c, w_ada, b_ada, g_mix, w_in, w_br_moba, w_br_sb, w_out, rel_bias, g_ffn, w_up,
           w_conv, b_conv, w_down, g_final):
    assert w_ada.shape[0] == 1, "the final rms_norm is fused into the single layer's last kernel"
    l = 0
    bias = _bias_tiles(rel_bias)
    mod = _ada(c, w_ada[l], b_ada[l]).reshape(x.shape[0], 6, D_MODEL)
    proj = _proj(x, mod, g_mix[l], w_in[l].astype(BF16))
    oa = _moba(proj, bias)
    ob = _sb(proj)
    return _post(x, oa, ob, proj, mod, g_ffn[l], g_final,
                 w_br_moba[l].astype(BF16), w_br_sb[l].astype(BF16), w_out[l].astype(BF16),
                 w_up[l].astype(BF16), w_conv[l], b_conv[l], w_down[l].astype(BF16))
```

```python
import functools
import math

import numpy as np
import jax
import jax.numpy as jnp
from jax import lax
from jax.experimental import pallas as pl
from jax.experimental.pallas import tpu as pltpu

D_MODEL = 1024
HEAD_DIM = 64
N_HEADS = 8
MIX_WIDTH = N_HEADS * HEAD_DIM
IN_WIDTH = 6 * MIX_WIDTH + 2 * D_MODEL
MOBA_BLOCK = 256
MOBA_TOPK = 3
D_FF = 2816
REL_BUCKETS = 32
REL_MAX_DIST = 128
NORM_EPS = 1e-6
NEG_INF = -1e30
LOG2E = math.log2(math.e)

LANES = 128
PAIR = LANES // HEAD_DIM
N_PAIRS = N_HEADS // PAIR
BLK = MOBA_BLOCK
VMEM_LIMIT = 56 * 1024 * 1024

F32 = jnp.float32
BF16 = jnp.bfloat16


def _dot(a, b):
    return jnp.dot(a, b, preferred_element_type=F32)


def _dot_nt(a, b):
    return lax.dot_general(a, b, (((1,), (1,)), ((), ())), preferred_element_type=F32)


def _rms(x):
    return x * lax.rsqrt(jnp.mean(x * x, axis=-1, keepdims=True) + NORM_EPS)


def _ada_kernel(c_ref, w_ref, b_ref, o_ref):
    c = c_ref[...]
    ca = c * jax.nn.sigmoid(c)
    o_ref[...] = jnp.dot(ca, w_ref[...], precision=lax.Precision.HIGHEST,
                         preferred_element_type=F32) + b_ref[...]


def _ada(c, w_ada, b_ada):
    b, d = c.shape
    n = w_ada.shape[1]
    tn = D_MODEL
    return pl.pallas_call(
        _ada_kernel,
        grid=(n // tn,),
        in_specs=[pl.BlockSpec((b, d), lambda j: (0, 0)),
                  pl.BlockSpec((d, tn), lambda j: (0, j)),
                  pl.BlockSpec((1, tn), lambda j: (0, j))],
        out_specs=pl.BlockSpec((b, tn), lambda j: (0, j)),
        out_shape=jax.ShapeDtypeStruct((b, n), F32),
        compiler_params=pltpu.CompilerParams(dimension_semantics=("arbitrary",),
                                             vmem_limit_bytes=VMEM_LIMIT),
        name="ada",
    )(c, w_ada, b_ada.reshape(1, n))


def _bucket_tiles():
    def bucket(rel):
        n = np.maximum(rel, 0)
        max_exact = REL_BUCKETS // 2
        n_f = np.maximum(n, max_exact).astype(np.float32)
        large = max_exact + (np.log(n_f / np.float32(max_exact)) / np.float32(math.log(REL_MAX_DIST / max_exact))
                             * np.float32(REL_BUCKETS - max_exact)).astype(np.int32)
        large = np.minimum(large, REL_BUCKETS - 1)
        return np.where(n < max_exact, n, large).astype(np.int32)
    qi = np.arange(BLK)[:, None]
    kj = np.arange(BLK)[None, :]
    t0 = np.where(qi >= kj, bucket(qi - kj), REL_BUCKETS)
    t1 = bucket(qi - kj + BLK)
    far = bucket(np.array([2 * BLK - (BLK - 1)]))
    assert int(far[0]) == REL_BUCKETS - 1
    return np.stack([t0, t1]).astype(np.int32)


def _bias_kernel(tab_ref, bkt_ref, o_ref):
    h = pl.program_id(0)
    for t in range(2):
        b = bkt_ref[t]
        acc = jnp.full((BLK, BLK), NEG_INF, F32)
        for k in range(REL_BUCKETS):
            acc = jnp.where(b == k, tab_ref[h, k] * LOG2E, acc)
        o_ref[0, t] = acc
    o_ref[0, 2] = jnp.full((BLK, BLK), tab_ref[h, REL_BUCKETS - 1] * LOG2E, F32)


def _bias_tiles(rel_bias):
    bkt = jnp.asarray(_bucket_tiles())
    return pl.pallas_call(
        _bias_kernel,
        grid=(N_HEADS,),
        in_specs=[pl.BlockSpec(memory_space=pltpu.SMEM),
                  pl.BlockSpec((2, BLK, BLK), lambda h: (0, 0, 0))],
        out_specs=pl.BlockSpec((1, 3, BLK, BLK), lambda h: (h, 0, 0, 0)),
        out_shape=jax.ShapeDtypeStruct((N_HEADS, 3, BLK, BLK), F32),
        compiler_params=pltpu.CompilerParams(dimension_semantics=("arbitrary",)),
        name="bias_tiles",
    )(rel_bias, bkt)


PROJ_TS = 512
PROJ_SPLIT = 2
PROJ_NC = 512


def _proj_kernel(x_ref, mod_ref, g_ref, w_ref, o_ref):
    ts = x_ref.shape[1]
    sh = mod_ref[0, 0:1, :]
    sc = mod_ref[0, 1:2, :]
    groups = [slice(i * ts // PROJ_SPLIT, (i + 1) * ts // PROJ_SPLIT) for i in range(PROJ_SPLIT)]
    hs = [(_rms(x_ref[0, rows, :]) * g_ref[...] * (1.0 + sc) + sh).astype(BF16) for rows in groups]
    scale = HEAD_DIM ** -0.5
    for n0 in range(0, IN_WIDTH, PROJ_NC):
        for rows, h in zip(groups, hs):
            r = _dot(h, w_ref[:, n0:n0 + PROJ_NC])
            if n0 in (0, 3 * MIX_WIDTH):
                r = r * (scale * LOG2E)
            elif n0 >= 6 * MIX_WIDTH:
                r = jax.nn.sigmoid(r)
            o_ref[0, rows, n0:n0 + PROJ_NC] = r.astype(BF16)


def _proj(x, mod, g_mix, w_in_bf16):
    b, s, d = x.shape
    ts = PROJ_TS
    return pl.pallas_call(
        _proj_kernel,
        grid=(b, s // ts),
        in_specs=[pl.BlockSpec((1, ts, d), lambda i, j: (i, j, 0)),
                  pl.BlockSpec((1, 6, d), lambda i, j: (i, 0, 0)),
                  pl.BlockSpec((1, d), lambda i, j: (0, 0)),
                  pl.BlockSpec((d, IN_WIDTH), lambda i, j: (0, 0), pipeline_mode=pl.Buffered(1))],
        out_specs=pl.BlockSpec((1, ts, IN_WIDTH), lambda i, j: (i, j, 0)),
        out_shape=jax.ShapeDtypeStruct((b, s, IN_WIDTH), BF16),
        compiler_params=pltpu.CompilerParams(dimension_semantics=("arbitrary", "arbitrary"),
                                             vmem_limit_bytes=VMEM_LIMIT),
        name="proj",
    )(x, mod, g_mix.reshape(1, d), w_in_bf16)


ATTN_UNROLL = 32


def _head_lanes(shape, h):
    lane = lax.broadcasted_iota(jnp.int32, shape, len(shape) - 1)
    return (lane >= h * HEAD_DIM) & (lane < (h + 1) * HEAD_DIM)


def _rows(blk):
    return slice(blk * BLK, (blk + 1) * BLK)


def _run_chains(chains, step):
    states = [None] * len(chains)
    for k in range(max(len(c) for c in chains)):
        for ci, chain in enumerate(chains):
            if k < len(chain):
                states[ci] = step(chain[k], states[ci])
    return states


def _pair_chains(t, n_blk):
    qis = (t, n_blk - 1 - t)
    return qis, [[(qi, j, h) for j in range(qi, -1, -1)] for qi in qis for h in range(PAIR)]


def _store_pair(o_ref, qi, out0, out1):
    o_ref[0, _rows(qi), :] = jnp.where(_head_lanes((BLK, LANES), 0), out0, out1).astype(o_ref.dtype)


def _moba_prepare(q_ref, k_ref, qaug_ref, kaug_ref, pen_ref):
    seq = k_ref.shape[1]
    n_blk = seq // BLK
    k = k_ref[0]
    q = q_ref[0]
    sub = 16
    r = lax.broadcasted_iota(jnp.int32, (sub, seq), 0)
    c = lax.broadcasted_iota(jnp.int32, (sub, seq), 1)
    avg = jnp.where(c // BLK == r, 1.0 / BLK, 0.0).astype(BF16)
    kmean = _dot(avg, k)
    row = lax.broadcasted_iota(jnp.int32, (seq, LANES), 0)
    lane = lax.broadcasted_iota(jnp.int32, (seq, LANES), 1)
    rr = lax.broadcasted_iota(jnp.int32, (BLK, BLK), 0)
    cc = lax.broadcasted_iota(jnp.int32, (BLK, BLK), 1)
    eye = jnp.where(rr == cc, 1.0, 0.0).astype(BF16)
    jidx = lax.broadcasted_iota(jnp.int32, (8, seq), 0)
    qblk = lax.broadcasted_iota(jnp.int32, (8, seq), 1) // BLK
    pen_ref[...] = jnp.zeros(pen_ref.shape, F32)
    for h in range(PAIR):
        ind0 = (1 - h) * HEAD_DIM
        ind = (lane - ind0) == (row // BLK)
        kaug_ref[h] = jnp.where(_head_lanes((seq, LANES), h), k, jnp.where(ind, 1.0, 0.0).astype(BF16))
        kmh = jnp.where(_head_lanes((sub, LANES), h), kmean, 0.0)
        hi = kmh.astype(BF16)
        lo = (kmh - hi.astype(F32)).astype(BF16)
        g8 = (_dot_nt(hi, q) + _dot_nt(lo, q))[0:8, :]
        cnt = jnp.zeros((8, seq), jnp.int32)
        for jp in range(n_blk):
            rowv = g8[jp:jp + 1, :]
            beats = (rowv > g8) | ((rowv == g8) & (jp < jidx))
            cnt = cnt + jnp.where(beats & (jp < qblk), 1, 0)
        sel = (jidx == qblk) | ((jidx < qblk) & (cnt < MOBA_TOPK))
        pen_ref[ind0:ind0 + 8, :] = jnp.where(sel, 0.0, NEG_INF)
    for b in range(n_blk):
        pen_cols = _dot_nt(eye, pen_ref[:, _rows(b)].astype(BF16)).astype(BF16)
        for h in range(PAIR):
            qaug_ref[h, _rows(b), :] = jnp.where(_head_lanes((BLK, LANES), h), q[_rows(b), :], pen_cols)


def _moba_pair(t, n_blk, v_ref, bias_ref, o_ref, qaug_ref, kaug_ref):
    def step(args, st):
        qi, j, h = args
        s = _dot_nt(qaug_ref[h, _rows(qi), :], kaug_ref[h, _rows(j), :]) + bias_ref[h, min(qi - j, 2)]
        m_blk = jnp.max(s, axis=-1, keepdims=True)
        vj = v_ref[0, _rows(j), :]
        if st is None:
            p = jnp.exp2(s - m_blk)
            return m_blk, jnp.sum(p, axis=-1, keepdims=True), _dot(p.astype(BF16), vj)
        m, l, acc = st
        m_new = jnp.maximum(m, m_blk)
        alpha = jnp.exp2(m - m_new)
        p = jnp.exp2(s - m_new)
        return (m_new, alpha * l + jnp.sum(p, axis=-1, keepdims=True),
                alpha * acc + _dot(p.astype(BF16), vj))

    qis, chains = _pair_chains(t, n_blk)
    states = _run_chains(chains, step)
    for n, qi in enumerate(qis):
        (_, l0, a0), (_, l1, a1) = states[PAIR * n], states[PAIR * n + 1]
        _store_pair(o_ref, qi, a0 / l0, a1 / l1)


def _moba_kernel(q_ref, k_ref, v_ref, bias_ref, o_ref, qaug_ref, kaug_ref, pen_ref):
    t = pl.program_id(2)
    n_blk = k_ref.shape[1] // BLK

    @pl.when(t == 0)
    def _prepare():
        _moba_prepare(q_ref, k_ref, qaug_ref, kaug_ref, pen_ref)

    for tt in range(n_blk // 2):
        pl.when(t == tt)(functools.partial(_moba_pair, tt, n_blk, v_ref, bias_ref, o_ref, qaug_ref, kaug_ref))


def _moba(proj, bias):
    b, s, _ = proj.shape
    n_blk = s // BLK
    assert s % BLK == 0 and n_blk % 2 == 0 and n_blk <= 8
    full = lambda off: pl.BlockSpec((1, s, LANES), lambda bi, p, t: (bi, 0, off + p))
    return pl.pallas_call(
        _moba_kernel,
        grid=(b, N_PAIRS, n_blk // 2),
        in_specs=[full(0), full(MIX_WIDTH // LANES), full(2 * MIX_WIDTH // LANES),
                  pl.BlockSpec((PAIR, 3, BLK, BLK), lambda bi, p, t: (p, 0, 0, 0))],
        out_specs=pl.BlockSpec((1, s, LANES), lambda bi, p, t: (bi, 0, p)),
        out_shape=jax.ShapeDtypeStruct((b, s, MIX_WIDTH), BF16),
        scratch_shapes=[pltpu.VMEM((PAIR, s, LANES), BF16),
                        pltpu.VMEM((PAIR, s, LANES), BF16),
                        pltpu.VMEM((LANES, s), F32)],
        compiler_params=pltpu.CompilerParams(dimension_semantics=("arbitrary",) * 3,
                                             vmem_limit_bytes=VMEM_LIMIT),
        name="moba",
    )(proj, proj, proj, bias)


SB_MASK = -1e9


def _sb_mask_tiles():
    r = np.arange(BLK)[:, None]
    c = np.arange(BLK)[None, :]
    tri = np.where(c < r, 0.0, SB_MASK)
    return np.stack([np.zeros((BLK, PAIR * BLK)), np.tile(tri, (1, PAIR))]).astype(np.float32)


def _sb_kernel(q_ref, k_ref, v_ref, mask_ref, o_ref, ks_ref, vs_ref, lk_ref, lb_ref, a_ref, acc_ref):
    seq = k_ref.shape[1]
    n_blk = seq // BLK
    n_tiles = n_blk * (n_blk + 1) // 2
    rr = lax.broadcasted_iota(jnp.int32, (BLK, BLK), 0)
    cc = lax.broadcasted_iota(jnp.int32, (BLK, BLK), 1)
    upper = jnp.where(rr > cc, 1.0, 0.0).astype(BF16)
    sign_bit = jnp.uint32(0x80000000)

    for h in range(PAIR):
        in_head = _head_lanes((BLK, LANES), h)
        for blk in range(n_blk):
            ks_ref[blk, h * BLK:(h + 1) * BLK, :] = jnp.where(in_head, k_ref[0, _rows(blk), :], jnp.zeros((), BF16))
            vs_ref[blk, h * BLK:(h + 1) * BLK, :] = jnp.where(in_head, v_ref[0, _rows(blk), :], jnp.zeros((), BF16))

    def rows(blk):
        return pl.ds(pl.multiple_of(blk * BLK, BLK), BLK)

    def cols(h):
        return slice(h * BLK, (h + 1) * BLK)

    def next_tile(t):
        qi, j = t
        last = j == 0
        return jnp.where(last, qi + 1, qi), jnp.where(last, qi + 1, j - 1)

    def stage1(t, slot, run):
        qi, j = t
        diag = qi == j
        z = _dot_nt(q_ref[0, rows(qi), :], ks_ref[j]) + mask_ref[jnp.where(diag, 1, 0)]
        neg_abs = lax.bitcast_convert_type(lax.bitcast_convert_type(z, jnp.uint32) | sign_bit, F32)
        sp = jnp.log(1.0 + jnp.exp2(neg_abs)) * LOG2E
        log_beta = jnp.minimum(z, 0.0) - sp
        log_keep = log_beta - z
        lk_ref[slot] = log_keep.astype(BF16)
        lb_ref[slot] = log_beta
        keep = jnp.where(diag, 0.0, 1.0)
        through, before = [], []
        for h in range(PAIR):
            prev = run[h] * keep
            before.append(prev)
            through.append(prev + jnp.sum(log_keep[:, cols(h)], axis=-1, keepdims=True))
        return through, before

    def stage2(slot, before):
        for h in range(PAIR):
            later = _dot(lk_ref[slot, :, cols(h)], upper)
            a_ref[slot, :, cols(h)] = jnp.exp2(lb_ref[slot, :, cols(h)] + before[h] + later).astype(BF16)

    def stage3(t, slot):
        qi, j = t
        pv = _dot(a_ref[slot], vs_ref[j])
        acc = jnp.where(qi == j, pv, acc_ref[...] + pv)
        acc_ref[...] = acc
        o_ref[0, rows(qi), :] = acc.astype(o_ref.dtype)

    zero = jnp.zeros((BLK, 1), F32)
    t0 = (jnp.int32(0), jnp.int32(0))
    t1 = next_tile(t0)
    run, before0 = stage1(t0, 0, [zero] * PAIR)
    run, before1 = stage1(t1, 1, run)
    stage2(0, before0)

    def pipeline_step(state, slot):
        t_cur, t_m1, t_m2, run, before_m1 = state
        run, before_cur = stage1(t_cur, slot, run)
        stage2(1 - slot, before_m1)
        stage3(t_m2, slot)
        return next_tile(t_cur), t_cur, t_m1, run, before_cur

    def body(_, state):
        for u in range(ATTN_UNROLL):
            state = pipeline_step(state, u % 2)
        return state

    assert ATTN_UNROLL % 2 == 0 and (n_tiles - 4) % ATTN_UNROLL == 0
    state = lax.fori_loop(0, (n_tiles - 4) // ATTN_UNROLL, body, (next_tile(t1), t1, t0, run, before1))
    state = pipeline_step(state, 0)
    _, t_m1, t_m2, _, before_m1 = pipeline_step(state, 1)
    stage2(1, before_m1)
    stage3(t_m2, 0)
    stage3(t_m1, 1)


def _sb(proj):
    b, s, _ = proj.shape
    assert s % BLK == 0
    base = 3 * MIX_WIDTH // LANES
    full = lambda off: pl.BlockSpec((1, s, LANES), lambda bi, p: (bi, 0, off + p))
    return pl.pallas_call(
        _sb_kernel,
        grid=(b, N_PAIRS),
        in_specs=[full(base), full(base + MIX_WIDTH // LANES), full(base + 2 * MIX_WIDTH // LANES),
                  pl.BlockSpec((2, BLK, PAIR * BLK), lambda bi, p: (0, 0, 0))],
        out_specs=pl.BlockSpec((1, s, LANES), lambda bi, p: (bi, 0, p)),
        out_shape=jax.ShapeDtypeStruct((b, s, MIX_WIDTH), BF16),
        scratch_shapes=[pltpu.VMEM((s // BLK, PAIR * BLK, LANES), BF16),
                        pltpu.VMEM((s // BLK, PAIR * BLK, LANES), BF16),
                        pltpu.VMEM((2, BLK, PAIR * BLK), BF16),
                        pltpu.VMEM((2, BLK, PAIR * BLK), F32),
                        pltpu.VMEM((2, BLK, PAIR * BLK), BF16),
                        pltpu.VMEM((BLK, LANES), F32)],
        compiler_params=pltpu.CompilerParams(dimension_semantics=("arbitrary",) * 2,
                                             vmem_limit_bytes=VMEM_LIMIT),
        name="sb",
    )(proj, proj, proj, jnp.asarray(_sb_mask_tiles()))


POST_TS = 512
POST_SPLIT = 2
MXU_TILE = 256
FF_SPLIT = (D_FF // MXU_TILE + 1) // 2 * MXU_TILE
FF_CHUNKS = ((0, FF_SPLIT), (FF_SPLIT, D_FF - FF_SPLIT))
HALO = 8


def _gelu_tanh(x):
    return 0.5 * x * (1.0 + jnp.tanh(math.sqrt(2.0 / math.pi) * (x + 0.044715 * (x * x * x))))


def _post_kernel(x_ref, oa_ref, ob_ref, ga_ref, gb_ref, mod_ref, gffn_ref, gfin_ref,
                 wbm_ref, wbs_ref, wout_ref, wup_ref, wconv_ref, bconv_ref, wdown_ref,
                 o_ref, halo_ref, ubuf_ref):
    ts = x_ref.shape[1]

    @pl.when(pl.program_id(1) == 0)
    def _zero_halo():
        halo_ref[...] = jnp.zeros_like(halo_ref)

    gt_m = mod_ref[0, 2:3, :]
    sh_f = mod_ref[0, 3:4, :]
    sc_f = mod_ref[0, 4:5, :]
    gt_f = mod_ref[0, 5:6, :]

    halves = [slice(i * ts // POST_SPLIT, (i + 1) * ts // POST_SPLIT) for i in range(POST_SPLIT)]
    x1, h2 = [], []
    for rows in halves:
        ya = _dot(oa_ref[0, rows, :], wbm_ref[...])
        yb = _dot(ob_ref[0, rows, :], wbs_ref[...])
        mix = (ga_ref[0, rows, :].astype(F32) * ya + gb_ref[0, rows, :].astype(F32) * yb).astype(BF16)
        x1.append(x_ref[0, rows, :] + gt_m * _dot(mix, wout_ref[...]))
        h2.append((_rms(x1[-1]) * gffn_ref[...] * (1.0 + sc_f) + sh_f).astype(BF16))

    def conv(c0, width, buf):
        cols = slice(c0, c0 + width)
        ubuf_ref[buf, 0:HALO, 0:width] = halo_ref[:, cols]
        for rows, h2_rows in zip(halves, h2):
            ubuf_ref[buf, HALO + rows.start:HALO + rows.stop, 0:width] = _dot(h2_rows, wup_ref[:, cols])
        u = ubuf_ref[buf, HALO:HALO + ts, 0:width]
        halo_ref[:, cols] = ubuf_ref[buf, ts:HALO + ts, 0:width]
        p1 = ubuf_ref[buf, HALO - 1:HALO - 1 + ts, 0:width]
        p2 = ubuf_ref[buf, HALO - 2:HALO - 2 + ts, 0:width]
        w = wconv_ref[:, cols]
        return w[0:1, :] * p2 + w[1:2, :] * p1 + w[2:3, :] * u + bconv_ref[:, cols]

    y = [jnp.zeros((ts // POST_SPLIT, D_MODEL), F32)] * POST_SPLIT
    for c0, width in FF_CHUNKS:
        u_val = conv(c0, width, 0)
        u_gate = conv(D_FF + c0, width, 1)
        act = (_gelu_tanh(u_gate) * u_val).astype(BF16)
        y = [y_rows + _dot(act[rows, :], wdown_ref[c0:c0 + width, :]) for rows, y_rows in zip(halves, y)]
    for rows, x1_rows, y_rows in zip(halves, x1, y):
        o_ref[0, rows, :] = _rms(x1_rows + gt_f * y_rows) * gfin_ref[...]


def _post(x, oa, ob, proj, mod, g_ffn, g_final, wbm, wbs, wout, wup, wconv, bconv, wdown):
    b, s, d = x.shape
    ts = POST_TS
    ga_blk = 6 * MIX_WIDTH // D_MODEL
    tok = lambda i, j: (i, j, 0)
    const2 = lambda i, j: (0, 0)
    resident = functools.partial(pl.BlockSpec, index_map=const2, pipeline_mode=pl.Buffered(1))
    return pl.pallas_call(
        _post_kernel,
        grid=(b, s // ts),
        in_specs=[pl.BlockSpec((1, ts, d), tok),
                  pl.BlockSpec((1, ts, MIX_WIDTH), tok),
                  pl.BlockSpec((1, ts, MIX_WIDTH), tok),
                  pl.BlockSpec((1, ts, d), lambda i, j: (i, j, ga_blk)),
                  pl.BlockSpec((1, ts, d), lambda i, j: (i, j, ga_blk + 1)),
                  pl.BlockSpec((1, 6, d), lambda i, j: (i, 0, 0)),
                  pl.BlockSpec((1, d), const2),
                  pl.BlockSpec((1, d), const2),
                  resident((MIX_WIDTH, d)),
                  resident((MIX_WIDTH, d)),
                  resident((d, d)),
                  resident((d, 2 * D_FF)),
                  pl.BlockSpec((3, 2 * D_FF), const2),
                  pl.BlockSpec((1, 2 * D_FF), const2),
                  resident((D_FF, d))],
        out_specs=pl.BlockSpec((1, ts, d), tok),
        out_shape=jax.ShapeDtypeStruct((b, s, d), x.dtype),
        scratch_shapes=[pltpu.VMEM((HALO, 2 * D_FF), F32),
                        pltpu.VMEM((2, HALO + ts, FF_SPLIT), F32)],
        compiler_params=pltpu.CompilerParams(dimension_semantics=("arbitrary", "arbitrary"),
                                             vmem_limit_bytes=VMEM_LIMIT),
        name="post",
    )(x, oa, ob, proj, proj, mod, g_ffn.reshape(1, d), g_final.reshape(1, d),
      wbm, wbs, wout, wup, wconv, bconv.reshape(1, 2 * D_FF), wdown)


@jax.jit
def kernel(x, c, w_ada, b_ada, g_mix, w_in, w_br_moba, w_br_sb, w_out, rel_bias, g_ffn, w_up,
           w_conv, b_conv, w_down, g_final):
    assert w_ada.shape[0] == 1, "the final rms_norm is fused into the single layer's last kernel"
    l = 0
    bias = _bias_tiles(rel_bias)
    mod = _ada(c, w_ada[l], b_ada[l]).reshape(x.shape[0], 6, D_MODEL)
    proj = _proj(x, mod, g_mix[l], w_in[l].astype(BF16))
    oa = _moba(proj, bias)
    ob = _sb(proj)
    return _post(x, oa, ob, proj, mod, g_ffn[l], g_final,
                 w_br_moba[l].astype(BF16), w_br_sb[l].astype(BF16), w_out[l].astype(BF16),
                 w_up[l].astype(BF16), w_conv[l], b_conv[l], w_down[l].astype(BF16))
```
